```python
import math
import jax
import jax.numpy as jnp
from jax import lax
import numpy as np

D_MODEL = 4096
BATCH = 1
SEQ = 16384
DEPTH = 2

N_A_LAYERS = DEPTH // 2
N_B_LAYERS = DEPTH - N_A_LAYERS
HG_HEADS = 32
HG_EXPAND = 128
HG_HEAD_I = D_MODEL // HG_HEADS
HG_FDIM = HG_HEADS * HG_EXPAND
HG_CHUNK = 64
DA_HEAD_DIM = 128
DA_HEADS = D_MODEL // (2 * DA_HEAD_DIM)
DA_QK_DIM = DA_HEADS * 2 * DA_HEAD_DIM
DA_V_DIM = DA_HEADS * 2 * DA_HEAD_DIM
DA_QBLOCK = 128
DA_QSUPER = 2048
REL_BUCKETS = 32
REL_MAX_DIST = 128
MOE_GROUPS = 8
MOE_PER_GROUP = 8
MOE_EXPERTS = MOE_GROUPS * MOE_PER_GROUP
MOE_TOPK = 2
MOE_FF = 384
MOE_BLOCK = 128
NORM_EPS = 1e-6

kernel_name = 'yoco_hgrn2_diffattn_hmoe'


def rmsnorm(x, gain):
    xf = x.astype(jnp.float32)
    y = xf * lax.rsqrt(jnp.mean(xf * xf, axis=-1, keepdims=True) + NORM_EPS)
    return (y * gain.astype(jnp.float32)).astype(x.dtype)


def hgrn2_mixer(xn, w_in, w_o, lower_bound, out_gain):
    B, S, D = xn.shape
    proj = xn @ w_in
    q, f, i, g = jnp.split(proj, [HG_FDIM, 2 * HG_FDIM, 2 * HG_FDIM + D], axis=-1)
    lb = lower_bound.astype(jnp.float32)
    q = jax.nn.silu(q.astype(jnp.float32))
    forget = lb + (1.0 - lb) * jax.nn.sigmoid(f.astype(jnp.float32))
    logf = jnp.log(forget)
    k = 1.0 - forget
    v = i.astype(jnp.float32)
    nc = S // HG_CHUNK

    def to_chunks(t, dim):
        return t.reshape(B, nc, HG_CHUNK, HG_HEADS, dim).transpose(1, 0, 3, 2, 4)

    qc, kc, fc = to_chunks(q, HG_EXPAND), to_chunks(k, HG_EXPAND), to_chunks(logf, HG_EXPAND)
    vc = to_chunks(v, HG_HEAD_I)
    causal = jnp.tril(jnp.ones((HG_CHUNK, HG_CHUNK), dtype=bool))[None, None, :, :, None]

    def step(state, inp):
        qt, kt, ft, vt = inp
        b = jnp.cumsum(ft, axis=2)
        diff = b[:, :, :, None, :] - b[:, :, None, :, :]
        decay = jnp.exp(jnp.where(causal, diff, -jnp.inf))
        scores = jnp.einsum('bhtk,bhtsk,bhsk->bhts', qt, decay, kt)
        o = (jnp.einsum('bhts,bhsv->bhtv', scores, vt)
             + jnp.einsum('bhtk,bhkv->bhtv', qt * jnp.exp(b), state))
        b_end = b[:, :, -1:, :]
        state = (jnp.exp(b_end[:, :, 0, :, None]) * state
                 + jnp.einsum('bhsk,bhsv->bhkv', kt * jnp.exp(b_end - b), vt))
        return state, o

    s0 = jnp.zeros((B, HG_HEADS, HG_EXPAND, HG_HEAD_I), jnp.float32)
    _, o = lax.scan(step, s0, (qc, kc, fc, vc))
    o = o.transpose(1, 0, 3, 2, 4).reshape(B, S, HG_HEADS, HG_HEAD_I)
    gate = jax.nn.silu(g.astype(jnp.float32)).reshape(B, S, HG_HEADS, HG_HEAD_I)
    o = rmsnorm(o, out_gain) * gate
    return o.reshape(B, S, D).astype(xn.dtype) @ w_o


def shared_kv(h, kv_gain, w_kv):
    B, S, _ = h.shape
    kv = rmsnorm(h, kv_gain) @ w_kv
    k, v = jnp.split(kv, [DA_QK_DIM], axis=-1)
    return (k.reshape(B, S, DA_HEADS, 2, DA_HEAD_DIM),
            v.reshape(B, S, DA_HEADS, 2 * DA_HEAD_DIM))


def t5_bucket(dist):
    max_exact = REL_BUCKETS // 2
    d = jnp.maximum(dist, 1).astype(jnp.float32)
    large = max_exact + (jnp.log(d / max_exact) / math.log(REL_MAX_DIST / max_exact)
                         * (REL_BUCKETS - max_exact)).astype(jnp.int32)
    large = jnp.minimum(large, REL_BUCKETS - 1)
    return jnp.where(dist < max_exact, dist, large)


def diff_attention(xn, k, v, w_q, lq1, lk1, lq2, lk2, subln_gain, w_o, rel_bias, lambda_init):
    B, S, D = xn.shape
    q = (xn @ w_q).reshape(B, S, DA_HEADS, 2, DA_HEAD_DIM) * (DA_HEAD_DIM ** -0.5)
    lam = (jnp.exp(jnp.sum(lq1.astype(jnp.float32) * lk1.astype(jnp.float32)))
           - jnp.exp(jnp.sum(lq2.astype(jnp.float32) * lk2.astype(jnp.float32))) + lambda_init)
    dist_bias = rel_bias.astype(jnp.float32)[t5_bucket(jnp.arange(S))].transpose(1, 2, 0)

    def attend_block(qb, q0, ks, vs):
        L = ks.shape[1]
        rel = (q0 + jnp.arange(DA_QBLOCK))[:, None] - jnp.arange(L)[None, :]
        s = (jnp.einsum('bqhmd,bkhmd->bhmqk', qb, ks).astype(jnp.float32)
             + dist_bias[:, :, jnp.maximum(rel, 0)][None])
        p = jax.nn.softmax(jnp.where(rel >= 0, s, -jnp.inf), axis=-1)
        a = p[:, :, 0] - lam * p[:, :, 1]
        return jnp.einsum('bhqk,bkhe->bqhe', a.astype(vs.dtype), vs)

    outs = []
    for s0 in range(0, S, DA_QSUPER):
        s1 = min(s0 + DA_QSUPER, S)
        nb = (s1 - s0) // DA_QBLOCK
        ks, vs = k[:, :s1], v[:, :s1]
        qs = q[:, s0:s1].reshape(B, nb, DA_QBLOCK, DA_HEADS, 2, DA_HEAD_DIM).transpose(1, 0, 2, 3, 4, 5)
        starts = s0 + DA_QBLOCK * jnp.arange(nb)
        ob = lax.map(lambda a: attend_block(a[0], a[1], ks, vs), (qs, starts))
        outs.append(ob.transpose(1, 0, 2, 3, 4).reshape(B, s1 - s0, DA_HEADS, 2 * DA_HEAD_DIM))
    o = jnp.concatenate(outs, axis=1)
    o = rmsnorm(o, subln_gain) * (1.0 - lambda_init)
    return o.reshape(B, S, DA_V_DIM).astype(xn.dtype) @ w_o


def hier_moe(xn, w_group, b_group, w_expert, b_expert, w_gate, w_up, w_down):
    B, S, D = xn.shape
    n_tok = B * S
    xt = xn.reshape(n_tok, D)
    g_prob = jax.nn.softmax((xt @ w_group).astype(jnp.float32) + b_group.astype(jnp.float32), axis=-1)
    g_w, g_idx = lax.top_k(g_prob, 1)
    e_logits = ((xt @ w_expert).astype(jnp.float32).reshape(n_tok, MOE_GROUPS, MOE_PER_GROUP)
                + b_expert.astype(jnp.float32))
    e_logits = jnp.take_along_axis(e_logits, g_idx[:, :, None], axis=1)[:, 0]
    e_w, e_idx = lax.top_k(jax.nn.softmax(e_logits, axis=-1), MOE_TOPK)
    weight = g_w * e_w / jnp.sum(e_w, axis=-1, keepdims=True)
    expert = g_idx * MOE_PER_GROUP + e_idx
    n_asg = n_tok * MOE_TOPK
    flat_e = expert.reshape(n_asg)
    flat_t = jnp.repeat(jnp.arange(n_tok, dtype=jnp.int32), MOE_TOPK)
    flat_w = weight.reshape(n_asg)
    order = jnp.argsort(flat_e)
    sorted_e = flat_e[order]
    counts = jnp.bincount(flat_e, length=MOE_EXPERTS)
    starts = jnp.cumsum(counts) - counts
    padded = (counts + MOE_BLOCK - 1) // MOE_BLOCK * MOE_BLOCK
    pad_end = jnp.cumsum(padded)
    pad_start = pad_end - padded
    dest = pad_start[sorted_e] + jnp.arange(n_asg) - starts[sorted_e]
    n_blocks = -(-n_asg // MOE_BLOCK) + MOE_EXPERTS
    n_slots = n_blocks * MOE_BLOCK
    slot_tok = jnp.zeros((n_slots,), jnp.int32).at[dest].set(flat_t[order])
    slot_w = jnp.zeros((n_slots,), jnp.float32).at[dest].set(flat_w[order])
    block_e = jnp.minimum(jnp.searchsorted(pad_end, jnp.arange(n_blocks) * MOE_BLOCK, side='right'),
                          MOE_EXPERTS - 1)

    def expert_block(args):
        e, tok, w = args
        xb = xt[tok]
        hid = jax.nn.silu(xb @ w_gate[e]) * (xb @ w_up[e])
        return (hid @ w_down[e]) * w[:, None].astype(xb.dtype)

    y = lax.map(expert_block, (block_e, slot_tok.reshape(n_blocks, MOE_BLOCK),
                               slot_w.reshape(n_blocks, MOE_BLOCK)))
    out = jnp.zeros((n_tok, D), xt.dtype).at[slot_tok].add(y.reshape(n_slots, D))
    return out.reshape(B, S, D)


def setup_inputs(seed: int = 0) -> dict:
    key = jax.random.key(seed)
    ks = jax.random.split(key, 26)
    D = D_MODEL

    def nrm(k, shape, scale):
        return jax.random.normal(k, shape, jnp.float32) * scale

    def gain(k, shape):
        return 1.0 + 0.02 * jax.random.normal(k, shape, jnp.float32)

    return {
        'x': nrm(ks[0], (BATCH, SEQ, D), 1.0),
        'ln_mix': gain(ks[1], (DEPTH, D)),
        'ln_ffn': gain(ks[2], (DEPTH, D)),
        'hg_w_in': nrm(ks[3], (N_A_LAYERS, D, 2 * HG_FDIM + 2 * D), D ** -0.5),
        'hg_lower_bound': nrm(ks[4], (N_A_LAYERS + 1, HG_FDIM), 0.1),
        'hg_out_norm': gain(ks[5], (N_A_LAYERS, HG_HEAD_I)),
        'hg_w_o': nrm(ks[6], (N_A_LAYERS, D, D), D ** -0.5),
        'kv_norm': gain(ks[7], (D,)),
        'w_kv': nrm(ks[8], (D, DA_QK_DIM + DA_V_DIM), D ** -0.5),
        'da_w_q': nrm(ks[9], (N_B_LAYERS, D, DA_QK_DIM), D ** -0.5),
        'da_lambda_q1': nrm(ks[10], (N_B_LAYERS, DA_HEAD_DIM), 0.1),
        'da_lambda_k1': nrm(ks[11], (N_B_LAYERS, DA_HEAD_DIM), 0.1),
        'da_lambda_q2': nrm(ks[12], (N_B_LAYERS, DA_HEAD_DIM), 0.1),
        'da_lambda_k2': nrm(ks[13], (N_B_LAYERS, DA_HEAD_DIM), 0.1),
        'da_subln': gain(ks[14], (N_B_LAYERS, 2 * DA_HEAD_DIM)),
        'da_w_o': nrm(ks[15], (N_B_LAYERS, DA_V_DIM, D), DA_V_DIM ** -0.5),
        'rel_bias': nrm(ks[16], (REL_BUCKETS, DA_HEADS, 2), 0.2),
        'moe_w_group': nrm(ks[17], (DEPTH, D, MOE_GROUPS), D ** -0.5),
        'moe_b_group': nrm(ks[18], (DEPTH, MOE_GROUPS), 0.01),
        'moe_w_expert': nrm(ks[19], (DEPTH, D, MOE_GROUPS * MOE_PER_GROUP), D ** -0.5),
        'moe_b_expert': nrm(ks[20], (DEPTH, MOE_GROUPS, MOE_PER_GROUP), 0.01),
        'moe_w_gate': nrm(ks[21], (DEPTH, MOE_EXPERTS, D, MOE_FF), D ** -0.5),
        'moe_w_up': nrm(ks[22], (DEPTH, MOE_EXPERTS, D, MOE_FF), D ** -0.5),
        'moe_w_down': nrm(ks[23], (DEPTH, MOE_EXPERTS, MOE_FF, D), MOE_FF ** -0.5),
        'final_norm': gain(ks[24], (D,)),
    }


def reference(x, ln_mix, ln_ffn, hg_w_in, hg_lower_bound, hg_out_norm, hg_w_o, kv_norm, w_kv,
              da_w_q, da_lambda_q1, da_lambda_k1, da_lambda_q2, da_lambda_k2, da_subln, da_w_o,
              rel_bias, moe_w_group, moe_b_group, moe_w_expert, moe_b_expert, moe_w_gate,
              moe_w_up, moe_w_down, final_norm):
    lower_bounds = jnp.cumsum(jax.nn.softmax(hg_lower_bound.astype(jnp.float32), axis=0), axis=0)
    h = x
    k = v = None
    for layer in range(DEPTH):
        if layer < N_A_LAYERS:
            xn = rmsnorm(h, ln_mix[layer])
            h = h + hgrn2_mixer(xn, hg_w_in[layer], hg_w_o[layer], lower_bounds[layer],
                                hg_out_norm[layer])
        else:
            if layer == N_A_LAYERS:
                k, v = shared_kv(h, kv_norm, w_kv)
            j = layer - N_A_LAYERS
            lambda_init = 0.8 - 0.6 * math.exp(-0.3 * layer)
            xn = rmsnorm(h, ln_mix[layer])
            h = h + diff_attention(xn, k, v, da_w_q[j], da_lambda_q1[j], da_lambda_k1[j],
                                   da_lambda_q2[j], da_lambda_k2[j], da_subln[j], da_w_o[j],
                                   rel_bias, lambda_init)
        xn = rmsnorm(h, ln_ffn[layer])
        h = h + hier_moe(xn, moe_w_group[layer], moe_b_group[layer], moe_w_expert[layer],
                         moe_b_expert[layer], moe_w_gate[layer], moe_w_up[layer], moe_w_down[layer])
    return rmsnorm(h, final_norm)
```

```python
import functools
import math

import numpy as np
import jax
import jax.numpy as jnp
from jax import lax
from jax.experimental import pallas as pl
from jax.experimental.pallas import tpu as pltpu

F32 = jnp.float32
BF16 = jnp.bfloat16

NORM_EPS = 1e-6
HEAD_DIM = 128
HG_CHUNK = 64
HG_TILE = 256
MOE_GROUPS = 8
MOE_PER_GROUP = 8
MOE_EXPERTS = MOE_GROUPS * MOE_PER_GROUP
MOE_TOPK = 2
REL_BUCKETS = 32
REL_MAX_DIST = 128
NEG_BIG = -1e30
LOG2E = 1.4426950408889634
VMEM_LIMIT_BYTES = 56 * 1024 * 1024


def _params(*sem):
    return pltpu.CompilerParams(dimension_semantics=sem, vmem_limit_bytes=VMEM_LIMIT_BYTES)


def _sigmoid(x):
    return 1.0 / (1.0 + jnp.exp(-x))


def _dot(a, b):
    return jnp.dot(a, b, preferred_element_type=F32)


def _dot_nt(a, b):
    return lax.dot_general(a, b, (((1,), (1,)), ((), ())), preferred_element_type=F32)


def _dot_tn(a, b):
    return lax.dot_general(a, b, (((0,), (0,)), ((), ())), preferred_element_type=F32)


def _norm_kernel(*refs, n_in, n_gain, emit_sum):
    in_refs = refs[:n_in]
    g_ref = refs[n_in]
    out_refs = refs[n_in + 1:]
    x = in_refs[0][...].astype(F32)
    for r in in_refs[1:]:
        x = x + r[...].astype(F32)
    k = 0
    if emit_sum:
        out_refs[0][...] = x
        k = 1
    xn = x * lax.rsqrt(jnp.mean(x * x, axis=-1, keepdims=True) + NORM_EPS)
    for j in range(n_gain):
        out_refs[k + j][...] = (xn * g_ref[j:j + 1, :]).astype(out_refs[k + j].dtype)


def _norm(h, gains, out_dtypes, *, y=None, emit_sum=False, tr=256):
    S, D = h.shape
    n_gain = gains.shape[0]
    ins = [h]
    in_specs = [pl.BlockSpec((tr, D), lambda i: (i, 0))]
    if y is not None:
        ins += [y, y]
        in_specs += [pl.BlockSpec((None, tr, D), lambda i: (0, i, 0)),
                     pl.BlockSpec((None, tr, D), lambda i: (1, i, 0))]
    ins.append(gains)
    in_specs.append(pl.BlockSpec((n_gain, D), lambda i: (0, 0)))
    out_shape = []
    if emit_sum:
        out_shape.append(jax.ShapeDtypeStruct((S, D), F32))
    out_shape += [jax.ShapeDtypeStruct((S, D), dt) for dt in out_dtypes]
    out_specs = [pl.BlockSpec((tr, D), lambda i: (i, 0)) for _ in out_shape]
    return pl.pallas_call(
        functools.partial(_norm_kernel, n_in=len(ins) - 1, n_gain=n_gain, emit_sum=emit_sum),
        grid=(S // tr,), in_specs=in_specs, out_specs=out_specs, out_shape=out_shape,
        compiler_params=_params("parallel"), name="norm",
    )(*ins)


def _mm_kernel(*refs, has_res, scale):
    a_ref, w_ref = refs[0], refs[1]
    o_ref = refs[-1]
    acc = _dot(a_ref[...], w_ref[...])
    if scale is not None:
        acc = acc * scale
    if has_res:
        acc = acc + refs[2][...]
    o_ref[...] = acc.astype(o_ref.dtype)


def _matmul(a, w, *, out_dtype, res=None, scale=None, tm=1024, tn=512, name="matmul"):
    M, K = a.shape
    N = w.shape[1]
    tm, tn = min(tm, M), min(tn, N)
    ins = [a, w]
    in_specs = [pl.BlockSpec((tm, K), lambda i, j: (i, 0)),
                pl.BlockSpec((K, tn), lambda i, j: (0, j))]
    if res is not None:
        ins.append(res)
        in_specs.append(pl.BlockSpec((tm, tn), lambda i, j: (i, j)))
    return pl.pallas_call(
        functools.partial(_mm_kernel, has_res=res is not None, scale=scale),
        grid=(M // tm, N // tn), in_specs=in_specs,
        out_specs=pl.BlockSpec((tm, tn), lambda i, j: (i, j)),
        out_shape=jax.ShapeDtypeStruct((M, N), out_dtype),
        compiler_params=_params("parallel", "arbitrary"), name=name,
    )(*ins)


_HG_LEVELS = (32, 16, 8, 4, 2, 1)


def _hgrn_decay_matrix():
    C = HG_CHUNK
    t = np.arange(C)[:, None]
    j = np.arange(C)[None, :]
    blocks = [(j <= t), (j > t)]
    for w in _HG_LEVELS:
        r = (t // (2 * w)) * (2 * w) + w - 1
        upper = (t // w) % 2 == 1
        blocks.append(upper & (j > r) & (j <= t))
        blocks.append((~upper) & (j > t) & (j <= r))
    return np.concatenate(blocks, axis=0).astype(np.float32)


def _hgrn_kernel(q_ref, f_ref, v_ref, g_ref, lb_ref, gain_ref, m_ref, o_ref, st_ref):
    C = HG_CHUNK

    @pl.when(pl.program_id(1) == 0)
    def _():
        st_ref[...] = jnp.zeros_like(st_ref)

    lbr = lb_ref[...].astype(F32)
    lbe = jnp.exp(lbr - jnp.max(lbr, axis=0, keepdims=True))
    lb = lbe[0:1, :] / jnp.sum(lbe, axis=0, keepdims=True)
    gain = gain_ref[...].astype(F32)
    dec = m_ref[...]

    row = lax.broadcasted_iota(jnp.int32, (C, HEAD_DIM), 0)
    ti = lax.broadcasted_iota(jnp.int32, (C, C), 0)
    si = lax.broadcasted_iota(jnp.int32, (C, C), 1)

    for c in range(q_ref.shape[0] // C):
        sl = slice(c * C, (c + 1) * C)
        q = q_ref[sl, :]
        qf = q * _sigmoid(q)
        fg = lb + (1.0 - lb) * _sigmoid(f_ref[sl, :])
        logf = jnp.log(fg)
        kk = 1.0 - fg
        v = v_ref[sl, :].astype(BF16)
        hi = logf.astype(BF16)
        r1 = logf - hi.astype(F32)
        mid = r1.astype(BF16)
        lo = (r1 - mid.astype(F32)).astype(BF16)
        sums = _dot(dec, hi) + _dot(dec, mid) + _dot(dec, lo)
        b = sums[0:C]
        tail = sums[C:2 * C]

        a = jnp.where(ti == si, jnp.sum(qf * kk, axis=-1, keepdims=True), 0.0)
        for l, w in enumerate(_HG_LEVELS):
            dq = sums[(2 + 2 * l) * C:(3 + 2 * l) * C]
            dk = sums[(3 + 2 * l) * C:(4 + 2 * l) * C]
            upper = ((row // w) % 2) == 1
            qw = jnp.where(upper, qf * jnp.exp(dq), 0.0).astype(BF16)
            kw = jnp.where(upper, 0.0, kk * jnp.exp(dk)).astype(BF16)
            same = (ti // (2 * w)) == (si // (2 * w))
            a = a + jnp.where(same, _dot_nt(qw, kw), 0.0)

        st = st_ref[...]
        o = _dot(a.astype(BF16), v) + _dot_nt((qf * jnp.exp(b)).astype(BF16), st.astype(BF16))
        st_ref[...] = st * jnp.exp(b[C - 1:C, :]) + _dot_tn(v, (kk * jnp.exp(tail)).astype(BF16))

        g = g_ref[sl, :]
        on = o * lax.rsqrt(jnp.mean(o * o, axis=-1, keepdims=True) + NORM_EPS) * gain
        o_ref[sl, :] = (on * (g * _sigmoid(g))).astype(o_ref.dtype)


def _hgrn2(proj, lower_bound, out_gain):
    S = proj.shape[0]
    D = proj.shape[1] // 4
    H = D // HEAD_DIM
    T = min(HG_TILE, S)
    dec = jnp.asarray(_hgrn_decay_matrix(), BF16)
    blk = lambda off: pl.BlockSpec((T, HEAD_DIM), lambda h, i, off=off: (i, off + h))
    return pl.pallas_call(
        _hgrn_kernel,
        grid=(H, S // T),
        in_specs=[blk(0), blk(H), blk(2 * H), blk(3 * H),
                  pl.BlockSpec((lower_bound.shape[0], HEAD_DIM), lambda h, i: (0, h)),
                  pl.BlockSpec((1, HEAD_DIM), lambda h, i: (0, 0)),
                  pl.BlockSpec(dec.shape, lambda h, i: (0, 0))],
        out_specs=pl.BlockSpec((T, HEAD_DIM), lambda h, i: (i, h)),
        out_shape=jax.ShapeDtypeStruct((S, D), BF16),
        scratch_shapes=[pltpu.VMEM((HEAD_DIM, HEAD_DIM), F32)],
        compiler_params=_params("parallel", "arbitrary"), name="hgrn2",
    )(proj, proj, proj, proj, lower_bound, out_gain.reshape(1, HEAD_DIM), dec)


def _rep(x, n):
    return x if n == 1 else jnp.concatenate([x] * n, axis=1)


def _attn_kernel(q_ref, k_ref, v_ref, bias_ref, lam_ref, gain_ref, o_ref, m_ref, l_ref, acc_ref,
                 *, t, lambda_init):
    i = pl.program_id(1)
    dh = HEAD_DIM
    m_ref[...] = jnp.full_like(m_ref, NEG_BIG)
    l_ref[...] = jnp.zeros_like(l_ref)
    acc_ref[...] = jnp.zeros_like(acc_ref)

    def tile(j, bias_kind):
        r0 = pl.multiple_of(j * t, t)
        vt = v_ref[pl.ds(r0, t), :]
        for m in range(2):
            s = _dot_nt(q_ref[:, m * dh:(m + 1) * dh], k_ref[pl.ds(r0, t), m * dh:(m + 1) * dh])
            if bias_kind is not None:
                s = s + bias_ref[m, bias_kind]
            m_prev = m_ref[m]
            m_next = jnp.maximum(m_prev, jnp.max(s, axis=-1, keepdims=True))
            p = jnp.exp2(s - _rep(m_next, t // 128))
            alpha = jnp.exp2(m_prev - m_next)
            l_ref[m] = alpha * l_ref[m] + jnp.sum(p, axis=-1, keepdims=True)
            acc_ref[m] = acc_ref[m] * _rep(alpha, 2) + _dot(p.astype(BF16), vt)
            m_ref[m] = m_next

    def far(j, carry):
        tile(j, None)
        return carry

    lax.fori_loop(0, i - 1, far, 0)

    @pl.when(i >= 1)
    def _():
        tile(i - 1, 1)

    tile(i, 0)

    lam_p = lam_ref[...].astype(F32)
    lam = (jnp.exp(jnp.sum(lam_p[0:1] * lam_p[1:2], axis=-1, keepdims=True))
           - jnp.exp(jnp.sum(lam_p[2:3] * lam_p[3:4], axis=-1, keepdims=True)) + lambda_init)
    o0 = acc_ref[0] * _rep(1.0 / l_ref[0], 2)
    o1 = acc_ref[1] * _rep(1.0 / l_ref[1], 2)
    att = o0 - lam * o1
    att = att * lax.rsqrt(jnp.mean(att * att, axis=-1, keepdims=True) + NORM_EPS)
    o_ref[...] = (att * gain_ref[...].astype(F32) * (1.0 - lambda_init)).astype(o_ref.dtype)


def _t5_bucket(dist):
    max_exact = REL_BUCKETS // 2
    d = jnp.maximum(dist, 1).astype(F32)
    large = max_exact + (jnp.log(d / max_exact) / math.log(REL_MAX_DIST / max_exact)
                         * (REL_BUCKETS - max_exact)).astype(jnp.int32)
    large = jnp.minimum(large, REL_BUCKETS - 1)
    return jnp.where(dist < max_exact, dist, large)


def _attn_bias_tiles(rel_bias, t):
    table = rel_bias.astype(F32)[_t5_bucket(jnp.arange(2 * t))]
    table = (table - table[REL_MAX_DIST][None]) * LOG2E
    table = table.transpose(1, 2, 0)
    rel = jnp.arange(t)[:, None] - jnp.arange(t)[None, :]
    diag = jnp.where(rel >= 0, table[:, :, jnp.maximum(rel, 0)], NEG_BIG)
    sub = table[:, :, rel + t]
    return jnp.stack([diag, sub], axis=2)


def _diff_attention(q, kv, bias, lam_params, gain, lambda_init, *, t):
    S, D = q.shape
    hw = 2 * HEAD_DIM
    H = D // hw
    return pl.pallas_call(
        functools.partial(_attn_kernel, t=t, lambda_init=lambda_init),
        grid=(H, S // t),
        in_specs=[pl.BlockSpec((t, hw), lambda h, i: (i, h)),
                  pl.BlockSpec((S, hw), lambda h, i: (0, h)),
                  pl.BlockSpec((S, hw), lambda h, i: (0, H + h)),
                  pl.BlockSpec((None, 2, 2, t, t), lambda h, i: (h, 0, 0, 0, 0)),
                  pl.BlockSpec((4, HEAD_DIM), lambda h, i: (0, 0)),
                  pl.BlockSpec((1, hw), lambda h, i: (0, 0))],
        out_specs=pl.BlockSpec((t, hw), lambda h, i: (i, h)),
        out_shape=jax.ShapeDtypeStruct((S, D), BF16),
        scratch_shapes=[pltpu.VMEM((2, t, 128), F32), pltpu.VMEM((2, t, 128), F32),
                        pltpu.VMEM((2, t, hw), F32)],
        compiler_params=_params("parallel", "arbitrary"), name="diff_attn",
    )(q, kv, kv, bias, lam_params, gain.reshape(1, hw))


def _router_kernel(h_ref, g_ref, w_ref, b_ref, xn_ref, ids_ref, wts_ref):
    x = h_ref[...]
    xn = x * lax.rsqrt(jnp.mean(x * x, axis=-1, keepdims=True) + NORM_EPS) * g_ref[...]
    xn_ref[...] = xn
    xh = xn.astype(BF16)
    xl = (xn - xh.astype(F32)).astype(BF16)
    w = w_ref[...]
    wh = w.astype(BF16)
    wl = (w - wh.astype(F32)).astype(BF16)
    logits = _dot(xh, wh) + _dot(xh, wl) + _dot(xl, wh) + b_ref[...]

    lane = lax.broadcasted_iota(jnp.int32, logits.shape, 1).astype(F32)
    first = lambda hit: jnp.min(jnp.where(hit, lane, 1e9), axis=-1, keepdims=True)
    gl = jnp.where(lane < MOE_GROUPS, logits, NEG_BIG)
    gmax = jnp.max(gl, axis=-1, keepdims=True)
    g_w = 1.0 / jnp.sum(jnp.exp(gl - gmax), axis=-1, keepdims=True)
    g_idx = first(gl == gmax)
    lo = MOE_GROUPS + MOE_PER_GROUP * g_idx
    el = jnp.where((lane >= lo) & (lane < lo + MOE_PER_GROUP), logits, NEG_BIG)
    e1 = jnp.max(el, axis=-1, keepdims=True)
    i1 = first(el == e1)
    el2 = jnp.where(lane == i1, NEG_BIG, el)
    e2 = jnp.max(el2, axis=-1, keepdims=True)
    i2 = first(el2 == e2)
    r = jnp.exp(e2 - e1)
    w1 = g_w / (1.0 + r)
    w2 = g_w * r / (1.0 + r)
    ids = jnp.where(lane == 0, i1, jnp.where(lane == 1, i2, float(MOE_GROUPS))) - MOE_GROUPS
    ids_ref[...] = ids.astype(jnp.int32)
    wts_ref[...] = jnp.where(lane == 0, w1, jnp.where(lane == 1, w2, 0.0))


def _router(h, gain, w_group, b_group, w_expert, b_expert, *, tr=256):
    S, D = h.shape
    n_used = MOE_GROUPS + MOE_EXPERTS
    w = jnp.pad(jnp.concatenate([w_group, w_expert], axis=1).astype(F32), ((0, 0), (0, 128 - n_used)))
    b = jnp.pad(jnp.concatenate([b_group.reshape(-1), b_expert.reshape(-1)]).astype(F32),
                (0, 128 - n_used)).reshape(1, 128)
    row = lambda width: pl.BlockSpec((tr, width), lambda i: (i, 0))
    full = lambda shape: pl.BlockSpec(shape, lambda i: (0, 0))
    xn, ids, wts = pl.pallas_call(
        _router_kernel, grid=(S // tr,),
        in_specs=[row(D), full((1, D)), full((D, 128)), full((1, 128))],
        out_specs=[row(D), row(128), row(128)],
        out_shape=[jax.ShapeDtypeStruct((S, D), F32), jax.ShapeDtypeStruct((S, 128), jnp.int32),
                   jax.ShapeDtypeStruct((S, 128), F32)],
        compiler_params=_params("parallel"), name="moe_router",
    )(h, gain.reshape(1, D), w, b)
    return xn, ids[:, :MOE_TOPK], wts[:, :MOE_TOPK]


def _dispatch(expert, weight, bm, n_tok):
    n_asg = n_tok * MOE_TOPK
    flat_e = expert.reshape(n_asg)
    flat_w = weight.reshape(n_asg)
    order = jnp.argsort(flat_e)
    sorted_e = flat_e[order]
    counts = jnp.bincount(flat_e, length=MOE_EXPERTS)
    starts = jnp.cumsum(counts) - counts
    padded = (counts + bm - 1) // bm * bm
    pad_end = jnp.cumsum(padded)
    pad_start = pad_end - padded
    dest = pad_start[sorted_e] + jnp.arange(n_asg) - starts[sorted_e]
    n_blocks = -(-n_asg // bm) + MOE_EXPERTS
    n_slots = n_blocks * bm
    tok = (order // MOE_TOPK).astype(jnp.int32)
    choice = (order % MOE_TOPK).astype(jnp.int32)
    slot_src = jnp.zeros((n_slots,), jnp.int32).at[dest].set(tok)
    slot_dst = jnp.full((n_slots,), -1, jnp.int32).at[dest].set(choice * n_tok + tok)
    slot_w = jnp.zeros((n_slots,), F32).at[dest].set(flat_w[order])
    block_start = jnp.arange(n_blocks) * bm
    block_e = jnp.minimum(jnp.searchsorted(pad_end, block_start, side='right'), MOE_EXPERTS - 1)
    block_live = (block_start < pad_end[-1]).astype(jnp.int32)
    return (block_e.astype(jnp.int32), block_live, slot_src.reshape(n_blocks, 1, bm),
            slot_dst.reshape(n_blocks, 1, bm), slot_w.reshape(n_slots, 1), n_blocks)


def _expert_kernel(be_ref, live_ref, src_ref, dst_ref, w_ref, x_hbm, wg_ref, wu_ref, wd_ref, y_hbm,
                   xbuf, ybuf, gsem, ssem, *, bm):
    b = pl.program_id(0)

    def gather(r):
        return pltpu.make_async_copy(x_hbm.at[pl.ds(src_ref[0, 0, r], 1), :],
                                     xbuf.at[pl.ds(r, 1), :], gsem)

    def scatter(r):
        return pltpu.make_async_copy(ybuf.at[pl.ds(r, 1), :],
                                     y_hbm.at[pl.ds(dst_ref[0, 0, r], 1), :], ssem)

    def each(fn, real_only=False):
        def body(r, carry):
            if real_only:
                pl.when(dst_ref[0, 0, r] >= 0)(lambda: fn(r))
            else:
                fn(r)
            return carry
        lax.fori_loop(0, bm, body, 0)

    @pl.when(live_ref[b] == 1)
    def _():
        each(lambda r: gather(r).start())
        each(lambda r: gather(r).wait())
        xb = xbuf[...].astype(BF16)
        gate = _dot(xb, wg_ref[...])
        hid = (gate * _sigmoid(gate)) * _dot(xb, wu_ref[...])
        ybuf[...] = _dot(hid.astype(BF16), wd_ref[...]) * w_ref[...]
        each(lambda r: scatter(r).start(), real_only=True)
        each(lambda r: scatter(r).wait(), real_only=True)


def _moe_experts(xn, expert, weight, w_gate, w_up, w_down, *, bm):
    S, D = xn.shape
    FF = w_gate.shape[-1]
    be, live, src, dst, sw, n_blocks = _dispatch(expert, weight, bm, S)
    smem = lambda: pl.BlockSpec((1, 1, bm), lambda b, be, live: (b, 0, 0), memory_space=pltpu.SMEM)
    y = pl.pallas_call(
        functools.partial(_expert_kernel, bm=bm),
        grid_spec=pltpu.PrefetchScalarGridSpec(
            num_scalar_prefetch=2, grid=(n_blocks,),
            in_specs=[smem(), smem(),
                      pl.BlockSpec((bm, 1), lambda b, be, live: (b, 0)),
                      pl.BlockSpec(memory_space=pl.ANY),
                      pl.BlockSpec((None, D, FF), lambda b, be, live: (be[b], 0, 0)),
                      pl.BlockSpec((None, D, FF), lambda b, be, live: (be[b], 0, 0)),
                      pl.BlockSpec((None, FF, D), lambda b, be, live: (be[b], 0, 0))],
            out_specs=pl.BlockSpec(memory_space=pl.ANY),
            scratch_shapes=[pltpu.VMEM((bm, D), F32), pltpu.VMEM((bm, D), F32),
                            pltpu.SemaphoreType.DMA(()), pltpu.SemaphoreType.DMA(())]),
        out_shape=jax.ShapeDtypeStruct((MOE_TOPK * S, D), F32),
        compiler_params=_params("arbitrary"), name="moe_experts",
    )(be, live, src, dst, sw, xn, w_gate, w_up, w_down)
    return y.reshape(MOE_TOPK, S, D)


def _moe(h, gain, w_group, b_group, w_expert, b_expert, w_gate, w_up, w_down, *, bm):
    xn, expert, weight = _router(h, gain, w_group, b_group, w_expert, b_expert)
    return _moe_experts(xn, expert, weight, w_gate.astype(BF16), w_up.astype(BF16),
                        w_down.astype(BF16), bm=bm)


def _forward(x, ln_mix, ln_ffn, hg_w_in, hg_lower_bound, hg_out_norm, hg_w_o, kv_norm, w_kv,
             da_w_q, da_lambda_q1, da_lambda_k1, da_lambda_q2, da_lambda_k2, da_subln, da_w_o,
             rel_bias, moe_w_group, moe_b_group, moe_w_expert, moe_b_expert, moe_w_gate,
             moe_w_up, moe_w_down, final_norm, *, attn_tile, moe_block):
    B, S, D = x.shape
    assert B == 1 and ln_mix.shape[0] == 2 and hg_w_in.shape[0] == 1 and da_w_q.shape[0] == 1
    h = x.reshape(S, D)

    (xn,) = _norm(h, ln_mix[0:1], [BF16])
    proj = _matmul(xn, hg_w_in[0].astype(BF16), out_dtype=F32, name="hg_in_proj")
    mix = _hgrn2(proj, hg_lower_bound, hg_out_norm[0])
    h = _matmul(mix, hg_w_o[0].astype(BF16), out_dtype=F32, res=h, name="hg_out_proj")
    y = _moe(h, ln_ffn[0], moe_w_group[0], moe_b_group[0], moe_w_expert[0], moe_b_expert[0],
             moe_w_gate[0], moe_w_up[0], moe_w_down[0], bm=moe_block)

    layer = 1
    lambda_init = 0.8 - 0.6 * math.exp(-0.3 * layer)
    h, xkv, xq = _norm(h, jnp.stack([kv_norm, ln_mix[1]]), [BF16, BF16], y=y, emit_sum=True)
    kv = _matmul(xkv, w_kv.astype(BF16), out_dtype=BF16, name="kv_proj")
    q = _matmul(xq, da_w_q[0].astype(BF16), out_dtype=BF16, scale=HEAD_DIM ** -0.5 * LOG2E,
                name="q_proj")
    lam_params = jnp.stack([da_lambda_q1[0], da_lambda_k1[0], da_lambda_q2[0], da_lambda_k2[0]])
    att = _diff_attention(q, kv, _attn_bias_tiles(rel_bias, attn_tile), lam_params, da_subln[0],
                          lambda_init, t=attn_tile)
    h = _matmul(att, da_w_o[0].astype(BF16), out_dtype=F32, res=h, name="da_out_proj")
    y = _moe(h, ln_ffn[1], moe_w_group[1], moe_b_group[1], moe_w_expert[1], moe_b_expert[1],
             moe_w_gate[1], moe_w_up[1], moe_w_down[1], bm=moe_block)

    (out,) = _norm(h, final_norm.reshape(1, D), [F32], y=y)
    return out.reshape(B, S, D)


def kernel(x, ln_mix, ln_ffn, hg_w_in, hg_lower_bound, hg_out_norm, hg_w_o, kv_norm, w_kv, da_w_q, da_lambda_q1, da_lambda_k1, da_lambda_q2, da_lambda_k2, da_subln, da_w_o, rel_bias, moe_w_group, moe_b_group, moe_w_expert, moe_b_expert, moe_w_gate, moe_w_up, moe_w_down, final_norm):
    return _forward(x, ln_mix, ln_ffn, hg_w_in, hg_lower_bound, hg_out_norm, hg_w_o, kv_norm, w_kv,
                    da_w_q, da_lambda_q1, da_lambda_k1, da_lambda_q2, da_lambda_k2, da_subln,
                    da_w_o, rel_bias, moe_w_group, moe_b_group, moe_w_expert, moe_b_expert,
                    moe_w_gate, moe_w_up, moe_w_down, final_norm,
                    attn_tile=min(512, x.shape[1]), moe_block=256)
```

```python
import functools
import math

import numpy as np
import jax
import jax.numpy as jnp
from jax import lax
from jax.experimental import pallas as pl
from jax.experimental.pallas import tpu as pltpu

F32 = jnp.float32
BF16 = jnp.bfloat16

NORM_EPS = 1e-6
HEAD_DIM = 128
HG_CHUNK = 64
HG_TILE = 256
HG_HEADS_PER_STEP = 4
MOE_GROUPS = 8
MOE_PER_GROUP = 8
MOE_EXPERTS = MOE_GROUPS * MOE_PER_GROUP
MOE_TOPK = 2
REL_BUCKETS = 32
REL_MAX_DIST = 128
NEG_BIG = -1e30
LOG2E = 1.4426950408889634
VMEM_LIMIT_BYTES = 56 * 1024 * 1024


def _params(*sem):
    return pltpu.CompilerParams(dimension_semantics=sem, vmem_limit_bytes=VMEM_LIMIT_BYTES)


def _sigmoid(x):
    return 1.0 / (1.0 + jnp.exp(-x))


def _dot(a, b):
    return jnp.dot(a, b, preferred_element_type=F32)


def _dot_nt(a, b):
    return lax.dot_general(a, b, (((1,), (1,)), ((), ())), preferred_element_type=F32)


def _dot_tn(a, b):
    return lax.dot_general(a, b, (((0,), (0,)), ((), ())), preferred_element_type=F32)


def _norm_kernel(*refs, n_in, n_gain, emit_sum):
    in_refs = refs[:n_in]
    g_ref = refs[n_in]
    out_refs = refs[n_in + 1:]
    x = in_refs[0][...].astype(F32)
    for r in in_refs[1:]:
        x = x + r[...].astype(F32)
    k = 0
    if emit_sum:
        out_refs[0][...] = x
        k = 1
    xn = x * lax.rsqrt(jnp.mean(x * x, axis=-1, keepdims=True) + NORM_EPS)
    for j in range(n_gain):
        out_refs[k + j][...] = (xn * g_ref[j:j + 1, :]).astype(out_refs[k + j].dtype)


def _norm(h, gains, out_dtypes, *, y=None, emit_sum=False, tr=256):
    S, D = h.shape
    n_gain = gains.shape[0]
    ins = [h]
    in_specs = [pl.BlockSpec((tr, D), lambda i: (i, 0))]
    if y is not None:
        ins += [y, y]
        in_specs += [pl.BlockSpec((None, tr, D), lambda i: (0, i, 0)),
                     pl.BlockSpec((None, tr, D), lambda i: (1, i, 0))]
    ins.append(gains)
    in_specs.append(pl.BlockSpec((n_gain, D), lambda i: (0, 0)))
    out_shape = []
    if emit_sum:
        out_shape.append(jax.ShapeDtypeStruct((S, D), F32))
    out_shape += [jax.ShapeDtypeStruct((S, D), dt) for dt in out_dtypes]
    out_specs = [pl.BlockSpec((tr, D), lambda i: (i, 0)) for _ in out_shape]
    return pl.pallas_call(
        functools.partial(_norm_kernel, n_in=len(ins) - 1, n_gain=n_gain, emit_sum=emit_sum),
        grid=(S // tr,), in_specs=in_specs, out_specs=out_specs, out_shape=out_shape,
        compiler_params=_params("parallel"), name="norm",
    )(*ins)


def _mm_kernel(*refs, has_res, scale):
    a_ref, w_ref = refs[0], refs[1]
    o_ref = refs[-1]
    acc = _dot(a_ref[...], w_ref[...])
    if scale is not None:
        acc = acc * scale
    if has_res:
        acc = acc + refs[2][...]
    o_ref[...] = acc.astype(o_ref.dtype)


def _matmul(a, w, *, out_dtype, res=None, scale=None, tm=1024, tn=512, name="matmul"):
    M, K = a.shape
    N = w.shape[1]
    tm, tn = min(tm, M), min(tn, N)
    ins = [a, w]
    in_specs = [pl.BlockSpec((tm, K), lambda i, j: (i, 0)),
                pl.BlockSpec((K, tn), lambda i, j: (0, j))]
    if res is not None:
        ins.append(res)
        in_specs.append(pl.BlockSpec((tm, tn), lambda i, j: (i, j)))
    return pl.pallas_call(
        functools.partial(_mm_kernel, has_res=res is not None, scale=scale),
        grid=(M // tm, N // tn), in_specs=in_specs,
        out_specs=pl.BlockSpec((tm, tn), lambda i, j: (i, j)),
        out_shape=jax.ShapeDtypeStruct((M, N), out_dtype),
        compiler_params=_params("parallel", "arbitrary"), name=name,
    )(*ins)


_HG_LEVELS = (32, 16, 8, 4, 2, 1)


def _hgrn_decay_matrix():
    C = HG_CHUNK
    t = np.arange(C)[:, None]
    j = np.arange(C)[None, :]
    blocks = [(j <= t), (j > t)]
    for w in _HG_LEVELS:
        r = (t // (2 * w)) * (2 * w) + w - 1
        upper = (t // w) % 2 == 1
        blocks.append(np.where(upper, (j > r) & (j <= t), (j > t) & (j <= r)))
    return np.concatenate(blocks, axis=0).astype(np.float32)


def _hgrn_kernel(q_ref, f_ref, v_ref, g_ref, lb_ref, gain_ref, m_ref, o_ref, st_ref, *, heads):
    C = HG_CHUNK
    dk = HEAD_DIM

    @pl.when(pl.program_id(1) == 0)
    def _():
        st_ref[...] = jnp.zeros_like(st_ref)

    lbr = lb_ref[...].astype(F32)
    lbe = jnp.exp(lbr - jnp.max(lbr, axis=0, keepdims=True))
    lb_all = lbe[0:1, :] / jnp.sum(lbe, axis=0, keepdims=True)
    gain = gain_ref[...].astype(F32)
    dec = m_ref[...]

    row = lax.broadcasted_iota(jnp.int32, (C, dk), 0)
    ti = lax.broadcasted_iota(jnp.int32, (C, C), 0)
    si = lax.broadcasted_iota(jnp.int32, (C, C), 1)
    upper = [((row // w) % 2) == 1 for w in _HG_LEVELS]
    same = [(ti // (2 * w)) == (si // (2 * w)) for w in _HG_LEVELS]

    for c in range(q_ref.shape[0] // C):
        sl = slice(c * C, (c + 1) * C)
        for hh in range(heads):
            cols = slice(hh * dk, (hh + 1) * dk)
            lb = lb_all[:, cols]
            q = q_ref[sl, cols]
            qf = q * _sigmoid(q)
            fg = lb + (1.0 - lb) * _sigmoid(f_ref[sl, cols])
            logf = jnp.log(fg)
            kk = 1.0 - fg
            v = v_ref[sl, cols].astype(BF16)
            hi = logf.astype(BF16)
            lo = (logf - hi.astype(F32)).astype(BF16)
            sums2 = _dot(dec, jnp.concatenate([hi, lo], axis=1))
            sums = sums2[:, :dk] + sums2[:, dk:]
            b = sums[0:C]
            tail = sums[C:2 * C]

            a = jnp.where(ti == si, jnp.sum(qf * kk, axis=-1, keepdims=True), 0.0)
            for l in range(len(_HG_LEVELS)):
                e = jnp.exp(sums[(2 + l) * C:(3 + l) * C])
                qw = jnp.where(upper[l], qf * e, 0.0).astype(BF16)
                kw = jnp.where(upper[l], 0.0, kk * e).astype(BF16)
                a = a + jnp.where(same[l], _dot_nt(qw, kw), 0.0)

            st = st_ref[hh]
            o = _dot(a.astype(BF16), v) + _dot_nt((qf * jnp.exp(b)).astype(BF16), st.astype(BF16))
            st_ref[hh] = st * jnp.exp(b[C - 1:C, :]) + _dot_tn(v, (kk * jnp.exp(tail)).astype(BF16))

            g = g_ref[sl, cols]
            on = o * lax.rsqrt(jnp.mean(o * o, axis=-1, keepdims=True) + NORM_EPS) * gain
            o_ref[sl, cols] = (on * (g * _sigmoid(g))).astype(o_ref.dtype)


def _hgrn2(proj, lower_bound, out_gain):
    S = proj.shape[0]
    D = proj.shape[1] // 4
    H = D // HEAD_DIM
    T = min(HG_TILE, S)
    hpb = min(HG_HEADS_PER_STEP, H)
    G = H // hpb
    w = hpb * HEAD_DIM
    dec = jnp.asarray(_hgrn_decay_matrix(), BF16)
    blk = lambda off: pl.BlockSpec((T, w), lambda h, i, off=off: (i, off + h))
    return pl.pallas_call(
        functools.partial(_hgrn_kernel, heads=hpb),
        grid=(G, S // T),
        in_specs=[blk(0), blk(G), blk(2 * G), blk(3 * G),
                  pl.BlockSpec((lower_bound.shape[0], w), lambda h, i: (0, h)),
                  pl.BlockSpec((1, HEAD_DIM), lambda h, i: (0, 0)),
                  pl.BlockSpec(dec.shape, lambda h, i: (0, 0))],
        out_specs=pl.BlockSpec((T, w), lambda h, i: (i, h)),
        out_shape=jax.ShapeDtypeStruct((S, D), BF16),
        scratch_shapes=[pltpu.VMEM((hpb, HEAD_DIM, HEAD_DIM), F32)],
        compiler_params=_params("parallel", "arbitrary"), name="hgrn2",
    )(proj, proj, proj, proj, lower_bound, out_gain.reshape(1, HEAD_DIM), dec)


def _rep(x, n):
    return x if n == 1 else jnp.concatenate([x] * n, axis=1)


def _attn_kernel(q_ref, k_ref, v_ref, bias_ref, lam_ref, gain_ref, o_ref, m_ref, l_ref, acc_ref,
                 *, t, lambda_init):
    i = pl.program_id(1)
    dh = HEAD_DIM
    m_ref[...] = jnp.full_like(m_ref, NEG_BIG)
    l_ref[...] = jnp.zeros_like(l_ref)
    acc_ref[...] = jnp.zeros_like(acc_ref)

    def tiles(specs):
        maps = range(2)
        rows = [pl.multiple_of(j * t, t) for j, _ in specs]
        scores = []
        for (j, kind), r0 in zip(specs, rows):
            s = [_dot_nt(q_ref[:, m * dh:(m + 1) * dh], k_ref[pl.ds(r0, t), m * dh:(m + 1) * dh])
                 for m in maps]
            if kind is not None:
                s = [s[m] + bias_ref[m, kind] for m in maps]
            scores.append(s)
        m_run = [m_ref[m] for m in maps]
        l_run = [l_ref[m] for m in maps]
        acc = [acc_ref[m] for m in maps]
        for s, r0 in zip(scores, rows):
            vt = v_ref[pl.ds(r0, t), :]
            for m in maps:
                m_next = jnp.maximum(m_run[m], jnp.max(s[m], axis=-1, keepdims=True))
                p = jnp.exp2(s[m] - _rep(m_next, t // 128))
                alpha = jnp.exp2(m_run[m] - m_next)
                l_run[m] = alpha * l_run[m] + jnp.sum(p, axis=-1, keepdims=True)
                acc[m] = acc[m] * _rep(alpha, 2) + _dot(p.astype(BF16), vt)
                m_run[m] = m_next
        for m in maps:
            m_ref[m] = m_run[m]
            l_ref[m] = l_run[m]
            acc_ref[m] = acc[m]

    n_far = jnp.maximum(i - 1, 0)

    def far_pair(jj, carry):
        tiles([(2 * jj, None), (2 * jj + 1, None)])
        return carry

    lax.fori_loop(0, n_far // 2, far_pair, 0)

    @pl.when(i == 0)
    def _():
        tiles([(i, 0)])

    @pl.when((i >= 1) & (n_far % 2 == 0))
    def _():
        tiles([(i - 1, 1), (i, 0)])

    @pl.when((i >= 1) & (n_far % 2 == 1))
    def _():
        tiles([(i - 2, None), (i - 1, 1), (i, 0)])

    lam_p = lam_ref[...].astype(F32)
    lam = (jnp.exp(jnp.sum(lam_p[0:1] * lam_p[1:2], axis=-1, keepdims=True))
           - jnp.exp(jnp.sum(lam_p[2:3] * lam_p[3:4], axis=-1, keepdims=True)) + lambda_init)
    o0 = acc_ref[0] * _rep(1.0 / l_ref[0], 2)
    o1 = acc_ref[1] * _rep(1.0 / l_ref[1], 2)
    att = o0 - lam * o1
    att = att * lax.rsqrt(jnp.mean(att * att, axis=-1, keepdims=True) + NORM_EPS)
    o_ref[...] = (att * gain_ref[...].astype(F32) * (1.0 - lambda_init)).astype(o_ref.dtype)


def _t5_bucket(dist):
    max_exact = REL_BUCKETS // 2
    d = jnp.maximum(dist, 1).astype(F32)
    large = max_exact + (jnp.log(d / max_exact) / math.log(REL_MAX_DIST / max_exact)
                         * (REL_BUCKETS - max_exact)).astype(jnp.int32)
    large = jnp.minimum(large, REL_BUCKETS - 1)
    return jnp.where(dist < max_exact, dist, large)


def _attn_bias_tiles(rel_bias, t):
    table = rel_bias.astype(F32)[_t5_bucket(jnp.arange(2 * t))]
    table = (table - table[REL_MAX_DIST][None]) * LOG2E
    table = table.transpose(1, 2, 0)
    rel = jnp.arange(t)[:, None] - jnp.arange(t)[None, :]
    diag = jnp.where(rel >= 0, table[:, :, jnp.maximum(rel, 0)], NEG_BIG)
    sub = table[:, :, rel + t]
    return jnp.stack([diag, sub], axis=2)


def _diff_attention(q, kv, bias, lam_params, gain, lambda_init, *, t):
    S, D = q.shape
    hw = 2 * HEAD_DIM
    H = D // hw
    return pl.pallas_call(
        functools.partial(_attn_kernel, t=t, lambda_init=lambda_init),
        grid=(H, S // t),
        in_specs=[pl.BlockSpec((t, hw), lambda h, i: (i, h)),
                  pl.BlockSpec((S, hw), lambda h, i: (0, h)),
                  pl.BlockSpec((S, hw), lambda h, i: (0, H + h)),
                  pl.BlockSpec((None, 2, 2, t, t), lambda h, i: (h, 0, 0, 0, 0)),
                  pl.BlockSpec((4, HEAD_DIM), lambda h, i: (0, 0)),
                  pl.BlockSpec((1, hw), lambda h, i: (0, 0))],
        out_specs=pl.BlockSpec((t, hw), lambda h, i: (i, h)),
        out_shape=jax.ShapeDtypeStruct((S, D), BF16),
        scratch_shapes=[pltpu.VMEM((2, t, 128), F32), pltpu.VMEM((2, t, 128), F32),
                        pltpu.VMEM((2, t, hw), F32)],
        compiler_params=_params("parallel", "arbitrary"), name="diff_attn",
    )(q, kv, kv, bias, lam_params, gain.reshape(1, hw))


def _router_kernel(h_ref, g_ref, w_ref, b_ref, xn_ref, ids_ref, wts_ref):
    x = h_ref[...]
    xn = x * lax.rsqrt(jnp.mean(x * x, axis=-1, keepdims=True) + NORM_EPS) * g_ref[...]
    xn_ref[...] = xn
    xh = xn.astype(BF16)
    xl = (xn - xh.astype(F32)).astype(BF16)
    w = w_ref[...]
    wh = w.astype(BF16)
    wl = (w - wh.astype(F32)).astype(BF16)
    logits = _dot(xh, wh) + _dot(xh, wl) + _dot(xl, wh) + b_ref[...]

    lane = lax.broadcasted_iota(jnp.int32, logits.shape, 1).astype(F32)
    first = lambda hit: jnp.min(jnp.where(hit, lane, 1e9), axis=-1, keepdims=True)
    gl = jnp.where(lane < MOE_GROUPS, logits, NEG_BIG)
    gmax = jnp.max(gl, axis=-1, keepdims=True)
    g_w = 1.0 / jnp.sum(jnp.exp(gl - gmax), axis=-1, keepdims=True)
    g_idx = first(gl == gmax)
    lo = MOE_GROUPS + MOE_PER_GROUP * g_idx
    el = jnp.where((lane >= lo) & (lane < lo + MOE_PER_GROUP), logits, NEG_BIG)
    e1 = jnp.max(el, axis=-1, keepdims=True)
    i1 = first(el == e1)
    el2 = jnp.where(lane == i1, NEG_BIG, el)
    e2 = jnp.max(el2, axis=-1, keepdims=True)
    i2 = first(el2 == e2)
    r = jnp.exp(e2 - e1)
    w1 = g_w / (1.0 + r)
    w2 = g_w * r / (1.0 + r)
    ids = jnp.where(lane == 0, i1, jnp.where(lane == 1, i2, float(MOE_GROUPS))) - MOE_GROUPS
    ids_ref[...] = ids.astype(jnp.int32)
    wts_ref[...] = jnp.where(lane == 0, w1, jnp.where(lane == 1, w2, 0.0))


def _router(h, gain, w_group, b_group, w_expert, b_expert, *, tr=256):
    S, D = h.shape
    n_used = MOE_GROUPS + MOE_EXPERTS
    w = jnp.pad(jnp.concatenate([w_group, w_expert], axis=1).astype(F32), ((0, 0), (0, 128 - n_used)))
    b = jnp.pad(jnp.concatenate([b_group.reshape(-1), b_expert.reshape(-1)]).astype(F32),
                (0, 128 - n_used)).reshape(1, 128)
    row = lambda width: pl.BlockSpec((tr, width), lambda i: (i, 0))
    full = lambda shape: pl.BlockSpec(shape, lambda i: (0, 0))
    xn, ids, wts = pl.pallas_call(
        _router_kernel, grid=(S // tr,),
        in_specs=[row(D), full((1, D)), full((D, 128)), full((1, 128))],
        out_specs=[row(D), row(128), row(128)],
        out_shape=[jax.ShapeDtypeStruct((S, D), F32), jax.ShapeDtypeStruct((S, 128), jnp.int32),
                   jax.ShapeDtypeStruct((S, 128), F32)],
        compiler_params=_params("parallel"), name="moe_router",
    )(h, gain.reshape(1, D), w, b)
    return xn, ids[:, :MOE_TOPK], wts[:, :MOE_TOPK]


def _dispatch(expert, weight, bm, n_tok):
    n_asg = n_tok * MOE_TOPK
    E = MOE_EXPERTS
    flat_e = expert.reshape(n_asg)
    onehot = (flat_e[:, None] == jnp.arange(E, dtype=jnp.int32)[None, :]).astype(jnp.int32)
    counts = jnp.sum(onehot, axis=0)
    rank = jnp.sum(jnp.cumsum(onehot, axis=0) * onehot, axis=1) - 1
    padded = (counts + bm - 1) // bm * bm
    pad_end = jnp.cumsum(padded)
    pad_start = pad_end - padded
    dest = jnp.sum(onehot * pad_start[None, :], axis=1) + rank
    n_blocks = -(-n_asg // bm) + E
    n_slots = n_blocks * bm
    n = jnp.arange(n_asg, dtype=jnp.int32)
    tok = n // MOE_TOPK
    row = (n % MOE_TOPK) * n_tok + tok
    w_bits = lax.bitcast_convert_type(weight.reshape(n_asg).astype(F32), jnp.int32)
    slots = jnp.zeros((n_slots, 3), jnp.int32).at[dest].set(jnp.stack([tok, row, w_bits], axis=1))
    block_start = jnp.arange(n_blocks, dtype=jnp.int32) * bm
    before = (pad_end[None, :] <= block_start[:, None]).astype(jnp.int32)
    block_e = jnp.minimum(jnp.sum(before, axis=1), E - 1)
    mine = (block_e[:, None] == jnp.arange(E, dtype=jnp.int32)[None, :]).astype(jnp.int32)
    real = jnp.clip(jnp.sum(mine * (pad_start + counts)[None, :], axis=1) - block_start, 0, bm)
    slot_w = lax.bitcast_convert_type(slots[:, 2], F32)
    return (block_e.astype(jnp.int32), real.astype(jnp.int32), slots[:, 0].reshape(n_blocks, 1, bm),
            slots[:, 1].reshape(n_blocks, 1, bm), slot_w.reshape(n_slots, 1), n_blocks)


_DMA_UNROLL = 8


def _expert_kernel(be_ref, real_ref, src_ref, nsrc_ref, dst_ref, w_ref, x_hbm, wgu_ref, wd_ref, y_hbm,
                   xbuf, ybuf, gsem, ssem, *, bm, ff):
    b = pl.program_id(0)
    nb = pl.num_programs(0)
    slot = b % 2
    n_here = real_ref[b]
    n_next = jnp.where(b + 1 < nb, real_ref[jnp.minimum(b + 1, nb - 1)], 0)
    n_prev = jnp.where(b >= 1, real_ref[jnp.maximum(b - 1, 0)], 0)

    def start_gather(idx_ref, buf_slot):
        def body(r, carry):
            pltpu.make_async_copy(x_hbm.at[pl.ds(idx_ref[0, 0, r], 1), :],
                                  xbuf.at[buf_slot, pl.ds(r, 1), :], gsem.at[buf_slot]).start()
            return carry
        lax.fori_loop(0, bm, body, 0, unroll=_DMA_UNROLL)

    def wait_group(sem, buf, size):
        pltpu.make_async_copy(buf.at[pl.ds(0, size), :], buf.at[pl.ds(0, size), :], sem).wait()

    def wait_rows(sem, buf, n):
        size = bm
        while size >= 1:
            pl.when((n & size) != 0)(functools.partial(wait_group, sem, buf, size))
            size //= 2

    @pl.when((b == 0) & (n_here > 0))
    def _():
        start_gather(src_ref, 0)

    @pl.when(n_here > 0)
    def _():
        @pl.when(n_next > 0)
        def _():
            start_gather(nsrc_ref, 1 - slot)

        wait_group(gsem.at[slot], xbuf.at[slot], bm)
        xb = xbuf[slot].astype(BF16)
        gu = _dot(xb, wgu_ref[...])
        gate = gu[:, :ff]
        hid = (gate * _sigmoid(gate)) * gu[:, ff:]
        ybuf[slot] = _dot(hid.astype(BF16), wd_ref[...]) * w_ref[...]

        wait_rows(ssem.at[1 - slot], ybuf.at[1 - slot], n_prev)

        def body(r, carry):
            pltpu.make_async_copy(ybuf.at[slot, pl.ds(r, 1), :],
                                  y_hbm.at[pl.ds(dst_ref[0, 0, r], 1), :], ssem.at[slot]).start()
            return carry
        lax.fori_loop(0, n_here, body, 0)

        @pl.when(n_next == 0)
        def _():
            wait_rows(ssem.at[slot], ybuf.at[slot], n_here)


def _moe_experts(xn, expert, weight, w_gate_up, w_down, *, bm):
    S, D = xn.shape
    FF = w_down.shape[1]
    be, real, src, dst, sw, n_blocks = _dispatch(expert, weight, bm, S)
    here = lambda b, be, real: (b, 0, 0)
    nxt = lambda b, be, real: (jnp.minimum(b + 1, n_blocks - 1), 0, 0)
    smem = lambda imap: pl.BlockSpec((1, 1, bm), imap, memory_space=pltpu.SMEM)
    y = pl.pallas_call(
        functools.partial(_expert_kernel, bm=bm, ff=FF),
        grid_spec=pltpu.PrefetchScalarGridSpec(
            num_scalar_prefetch=2, grid=(n_blocks,),
            in_specs=[smem(here), smem(nxt), smem(here),
                      pl.BlockSpec((bm, 1), lambda b, be, real: (b, 0)),
                      pl.BlockSpec(memory_space=pl.ANY),
                      pl.BlockSpec((None, D, 2 * FF), lambda b, be, real: (be[b], 0, 0)),
                      pl.BlockSpec((None, FF, D), lambda b, be, real: (be[b], 0, 0))],
            out_specs=pl.BlockSpec(memory_space=pl.ANY),
            scratch_shapes=[pltpu.VMEM((2, bm, D), F32), pltpu.VMEM((2, bm, D), F32),
                            pltpu.SemaphoreType.DMA((2,)), pltpu.SemaphoreType.DMA((2,))]),
        out_shape=jax.ShapeDtypeStruct((MOE_TOPK * S, D), F32),
        compiler_params=_params("arbitrary"), name="moe_experts",
    )(be, real, src, src, dst, sw, xn, w_gate_up, w_down)
    return y.reshape(MOE_TOPK, S, D)


def _moe(h, gain, w_group, b_group, w_expert, b_expert, w_gate, w_up, w_down, *, bm):
    xn, expert, weight = _router(h, gain, w_group, b_group, w_expert, b_expert)
    w_gate_up = jnp.concatenate([w_gate.astype(BF16), w_up.astype(BF16)], axis=-1)
    return _moe_experts(xn, expert, weight, w_gate_up, w_down.astype(BF16), bm=bm)


def _forward(x, ln_mix, ln_ffn, hg_w_in, hg_lower_bound, hg_out_norm, hg_w_o, kv_norm, w_kv,
             da_w_q, da_lambda_q1, da_lambda_k1, da_lambda_q2, da_lambda_k2, da_subln, da_w_o,
             rel_bias, moe_w_group, moe_b_group, moe_w_expert, moe_b_expert, moe_w_gate,
             moe_w_up, moe_w_down, final_norm, *, attn_tile, moe_block):
    B, S, D = x.shape
    assert B == 1 and ln_mix.shape[0] == 2 and hg_w_in.shape[0] == 1 and da_w_q.shape[0] == 1
    h = x.reshape(S, D)

    (xn,) = _norm(h, ln_mix[0:1], [BF16])
    proj = _matmul(xn, hg_w_in[0].astype(BF16), out_dtype=F32, name="hg_in_proj")
    mix = _hgrn2(proj, hg_lower_bound, hg_out_norm[0])
    h = _matmul(mix, hg_w_o[0].astype(BF16), out_dtype=F32, res=h, name="hg_out_proj")
    y = _moe(h, ln_ffn[0], moe_w_group[0], moe_b_group[0], moe_w_expert[0], moe_b_expert[0],
             moe_w_gate[0], moe_w_up[0], moe_w_down[0], bm=moe_block)

    layer = 1
    lambda_init = 0.8 - 0.6 * math.exp(-0.3 * layer)
    h, xkv, xq = _norm(h, jnp.stack([kv_norm, ln_mix[1]]), [BF16, BF16], y=y, emit_sum=True)
    kv = _matmul(xkv, w_kv.astype(BF16), out_dtype=BF16, name="kv_proj")
    q = _matmul(xq, da_w_q[0].astype(BF16), out_dtype=BF16, scale=HEAD_DIM ** -0.5 * LOG2E,
                name="q_proj")
    lam_params = jnp.stack([da_lambda_q1[0], da_lambda_k1[0], da_lambda_q2[0], da_lambda_k2[0]])
    att = _diff_attention(q, kv, _attn_bias_tiles(rel_bias, attn_tile), lam_params, da_subln[0],
                          lambda_init, t=attn_tile)
    h = _matmul(att, da_w_o[0].astype(BF16), out_dtype=F32, res=h, name="da_out_proj")
    y = _moe(h, ln_ffn[1], moe_w_group[1], moe_b_group[1], moe_w_expert[1], moe_b_expert[1],
             moe_w_gate[1], moe_w_up[1], moe_w_down[1], bm=moe_block)

    (out,) = _norm(h, final_norm.reshape(1, D), [F32], y=y)
    return out.reshape(B, S, D)


def kernel(x, ln_mix, ln_ffn, hg_w_in, hg_lower_bound, hg_out_norm, hg_w_o, kv_norm, w_kv, da_w_q, da_lambda_q1, da_lambda_k1, da_lambda_q2, da_lambda_k2, da_subln, da_w_o, rel_bias, moe_w_group, moe_b_group, moe_w_expert, moe_b_expert, moe_w_gate, moe_w_up, moe_w_down, final_norm):
    return _forward(x, ln_mix, ln_ffn, hg_w_in, hg_lower_bound, hg_out_norm, hg_w_o, kv_norm, w_kv,
                    da_w_q, da_lambda_q1, da_lambda_k1, da_lambda_q2, da_lambda_k2, da_subln,
                    da_w_o, rel_bias, moe_w_group, moe_b_group, moe_w_expert, moe_b_expert,
                    moe_w_gate, moe_w_up, moe_w_down, final_norm,
                    attn_tile=min(512, x.shape[1]), moe_block=256)
```

```python
import functools
import math

import numpy as np
import jax
import jax.numpy as jnp
from jax import lax
from jax.experimental import pallas as pl
from jax.experimental.pallas import tpu as pltpu

F32 = jnp.float32
BF16 = jnp.bfloat16

NORM_EPS = 1e-6
HEAD_DIM = 128
HG_CHUNK = 64
HG_TILE = 256
HG_HEADS_PER_STEP = 4
MOE_GROUPS = 8
MOE_PER_GROUP = 8
MOE_EXPERTS = MOE_GROUPS * MOE_PER_GROUP
MOE_TOPK = 2
EXPERT_ROW_PARTS = 2
REL_BUCKETS = 32
REL_MAX_DIST = 128
NEG_BIG = -1e30
LOG2E = 1.4426950408889634
VMEM_LIMIT_BYTES = 56 * 1024 * 1024


def _params(*sem):
    return pltpu.CompilerParams(dimension_semantics=sem, vmem_limit_bytes=VMEM_LIMIT_BYTES)


def _sigmoid(x):
    return 1.0 / (1.0 + jnp.exp(-x))


def _dot(a, b):
    return jnp.dot(a, b, preferred_element_type=F32)


def _dot_nt(a, b):
    return lax.dot_general(a, b, (((1,), (1,)), ((), ())), preferred_element_type=F32)


def _dot_tn(a, b):
    return lax.dot_general(a, b, (((0,), (0,)), ((), ())), preferred_element_type=F32)


def _norm_kernel(*refs, n_in, n_gain, emit_sum):
    in_refs = refs[:n_in]
    g_ref = refs[n_in]
    out_refs = refs[n_in + 1:]
    x = in_refs[0][...].astype(F32)
    for r in in_refs[1:]:
        x = x + r[...].astype(F32)
    k = 0
    if emit_sum:
        out_refs[0][...] = x
        k = 1
    xn = x * lax.rsqrt(jnp.mean(x * x, axis=-1, keepdims=True) + NORM_EPS)
    for j in range(n_gain):
        out_refs[k + j][...] = (xn * g_ref[j:j + 1, :]).astype(out_refs[k + j].dtype)


def _norm(h, gains, out_dtypes, *, y=None, emit_sum=False, tr=256):
    S, D = h.shape
    n_gain = gains.shape[0]
    ins = [h]
    in_specs = [pl.BlockSpec((tr, D), lambda i: (i, 0))]
    if y is not None:
        ins += [y, y]
        in_specs += [pl.BlockSpec((tr, D), lambda i: (i, 0)),
                     pl.BlockSpec((tr, D), lambda i: (S // tr + i, 0))]
    ins.append(gains)
    in_specs.append(pl.BlockSpec((n_gain, D), lambda i: (0, 0)))
    out_shape = []
    if emit_sum:
        out_shape.append(jax.ShapeDtypeStruct((S, D), F32))
    out_shape += [jax.ShapeDtypeStruct((S, D), dt) for dt in out_dtypes]
    out_specs = [pl.BlockSpec((tr, D), lambda i: (i, 0)) for _ in out_shape]
    return pl.pallas_call(
        functools.partial(_norm_kernel, n_in=len(ins) - 1, n_gain=n_gain, emit_sum=emit_sum),
        grid=(S // tr,), in_specs=in_specs, out_specs=out_specs, out_shape=out_shape,
        compiler_params=_params("parallel"), name="norm",
    )(*ins)


def _mm_kernel(*refs, has_res, scale):
    a_ref, w_ref = refs[0], refs[1]
    o_ref = refs[-1]
    acc = _dot(a_ref[...], w_ref[...])
    if scale is not None:
        acc = acc * scale
    if has_res:
        acc = acc + refs[2][...]
    o_ref[...] = acc.astype(o_ref.dtype)


def _matmul(a, w, *, out_dtype, res=None, scale=None, tm=1024, tn=512, name="matmul"):
    M, K = a.shape
    N = w.shape[1]
    tm, tn = min(tm, M), min(tn, N)
    ins = [a, w]
    in_specs = [pl.BlockSpec((tm, K), lambda i, j: (i, 0)),
                pl.BlockSpec((K, tn), lambda i, j: (0, j))]
    if res is not None:
        ins.append(res)
        in_specs.append(pl.BlockSpec((tm, tn), lambda i, j: (i, j)))
    return pl.pallas_call(
        functools.partial(_mm_kernel, has_res=res is not None, scale=scale),
        grid=(M // tm, N // tn), in_specs=in_specs,
        out_specs=pl.BlockSpec((tm, tn), lambda i, j: (i, j)),
        out_shape=jax.ShapeDtypeStruct((M, N), out_dtype),
        compiler_params=_params("parallel", "arbitrary"), name=name,
    )(*ins)


_HG_LEVELS = (32, 16, 8, 4, 2, 1)


def _hgrn_decay_matrix():
    C = HG_CHUNK
    t = np.arange(C)[:, None]
    j = np.arange(C)[None, :]
    blocks = [(j <= t), (j > t)]
    for w in _HG_LEVELS:
        r = (t // (2 * w)) * (2 * w) + w - 1
        upper = (t // w) % 2 == 1
        blocks.append(np.where(upper, (j > r) & (j <= t), (j > t) & (j <= r)))
    return np.concatenate(blocks, axis=0).astype(np.float32)


def _hgrn_kernel(q_ref, f_ref, v_ref, g_ref, lb_ref, gain_ref, m_ref, o_ref, st_ref, *, heads):
    C = HG_CHUNK
    dk = HEAD_DIM

    @pl.when(pl.program_id(1) == 0)
    def _():
        st_ref[...] = jnp.zeros_like(st_ref)

    lbr = lb_ref[...].astype(F32)
    lbe = jnp.exp(lbr - jnp.max(lbr, axis=0, keepdims=True))
    lb_all = lbe[0:1, :] / jnp.sum(lbe, axis=0, keepdims=True)
    gain = gain_ref[...].astype(F32)
    dec = m_ref[...]

    row = lax.broadcasted_iota(jnp.int32, (C, dk), 0)
    ti = lax.broadcasted_iota(jnp.int32, (C, C), 0)
    si = lax.broadcasted_iota(jnp.int32, (C, C), 1)
    upper = [((row // w) % 2) == 1 for w in _HG_LEVELS]
    same = [(ti // (2 * w)) == (si // (2 * w)) for w in _HG_LEVELS]

    for c in range(q_ref.shape[0] // C):
        sl = slice(c * C, (c + 1) * C)
        for hh in range(heads):
            cols = slice(hh * dk, (hh + 1) * dk)
            lb = lb_all[:, cols]
            q = q_ref[sl, cols]
            qf = q * _sigmoid(q)
            fg = lb + (1.0 - lb) * _sigmoid(f_ref[sl, cols])
            logf = jnp.log(fg)
            kk = 1.0 - fg
            v = v_ref[sl, cols].astype(BF16)
            hi = logf.astype(BF16)
            lo = (logf - hi.astype(F32)).astype(BF16)
            sums2 = _dot(dec, jnp.concatenate([hi, lo], axis=1))
            sums = sums2[:, :dk] + sums2[:, dk:]
            b = sums[0:C]
            tail = sums[C:2 * C]

            a = jnp.where(ti == si, jnp.sum(qf * kk, axis=-1, keepdims=True), 0.0)
            for l in range(len(_HG_LEVELS)):
                e = jnp.exp(sums[(2 + l) * C:(3 + l) * C])
                qw = jnp.where(upper[l], qf * e, 0.0).astype(BF16)
                kw = jnp.where(upper[l], 0.0, kk * e).astype(BF16)
                a = a + jnp.where(same[l], _dot_nt(qw, kw), 0.0)

            st = st_ref[hh]
            o = _dot(a.astype(BF16), v) + _dot_nt((qf * jnp.exp(b)).astype(BF16), st.astype(BF16))
            st_ref[hh] = st * jnp.exp(b[C - 1:C, :]) + _dot_tn(v, (kk * jnp.exp(tail)).astype(BF16))

            g = g_ref[sl, cols]
            on = o * lax.rsqrt(jnp.mean(o * o, axis=-1, keepdims=True) + NORM_EPS) * gain
            o_ref[sl, cols] = (on * (g * _sigmoid(g))).astype(o_ref.dtype)


def _hgrn2(proj, lower_bound, out_gain):
    S = proj.shape[0]
    D = proj.shape[1] // 4
    H = D // HEAD_DIM
    T = min(HG_TILE, S)
    hpb = min(HG_HEADS_PER_STEP, H)
    G = H // hpb
    w = hpb * HEAD_DIM
    dec = jnp.asarray(_hgrn_decay_matrix(), BF16)
    blk = lambda off: pl.BlockSpec((T, w), lambda h, i, off=off: (i, off + h))
    return pl.pallas_call(
        functools.partial(_hgrn_kernel, heads=hpb),
        grid=(G, S // T),
        in_specs=[blk(0), blk(G), blk(2 * G), blk(3 * G),
                  pl.BlockSpec((lower_bound.shape[0], w), lambda h, i: (0, h)),
                  pl.BlockSpec((1, HEAD_DIM), lambda h, i: (0, 0)),
                  pl.BlockSpec(dec.shape, lambda h, i: (0, 0))],
        out_specs=pl.BlockSpec((T, w), lambda h, i: (i, h)),
        out_shape=jax.ShapeDtypeStruct((S, D), BF16),
        scratch_shapes=[pltpu.VMEM((hpb, HEAD_DIM, HEAD_DIM), F32)],
        compiler_params=_params("parallel", "arbitrary"), name="hgrn2",
    )(proj, proj, proj, proj, lower_bound, out_gain.reshape(1, HEAD_DIM), dec)


def _rep(x, n):
    return x if n == 1 else jnp.concatenate([x] * n, axis=1)


def _attn_kernel(q_ref, k_ref, v_ref, bias_ref, lam_ref, gain_ref, o_ref, m_ref, l_ref, acc_ref,
                 *, t, lambda_init):
    i = pl.program_id(1)
    dh = HEAD_DIM
    m_ref[...] = jnp.full_like(m_ref, NEG_BIG)
    l_ref[...] = jnp.zeros_like(l_ref)
    acc_ref[...] = jnp.zeros_like(acc_ref)

    def tiles(specs):
        maps = range(2)
        rows = [pl.multiple_of(j * t, t) for j, _ in specs]
        scores = []
        for (j, kind), r0 in zip(specs, rows):
            s = [_dot_nt(q_ref[:, m * dh:(m + 1) * dh], k_ref[pl.ds(r0, t), m * dh:(m + 1) * dh])
                 for m in maps]
            if kind is not None:
                s = [s[m] + bias_ref[m, kind] for m in maps]
            scores.append(s)
        m_run = [m_ref[m] for m in maps]
        l_run = [l_ref[m] for m in maps]
        acc = [acc_ref[m] for m in maps]
        for s, r0 in zip(scores, rows):
            vt = v_ref[pl.ds(r0, t), :]
            for m in maps:
                m_next = jnp.maximum(m_run[m], jnp.max(s[m], axis=-1, keepdims=True))
                p = jnp.exp2(s[m] - _rep(m_next, t // 128))
                alpha = jnp.exp2(m_run[m] - m_next)
                l_run[m] = alpha * l_run[m] + jnp.sum(p, axis=-1, keepdims=True)
                acc[m] = acc[m] * _rep(alpha, 2) + _dot(p.astype(BF16), vt)
                m_run[m] = m_next
        for m in maps:
            m_ref[m] = m_run[m]
            l_ref[m] = l_run[m]
            acc_ref[m] = acc[m]

    n_far = jnp.maximum(i - 1, 0)

    def far_pair(jj, carry):
        tiles([(2 * jj, None), (2 * jj + 1, None)])
        return carry

    lax.fori_loop(0, n_far // 2, far_pair, 0)

    @pl.when(i == 0)
    def _():
        tiles([(i, 0)])

    @pl.when((i >= 1) & (n_far % 2 == 0))
    def _():
        tiles([(i - 1, 1), (i, 0)])

    @pl.when((i >= 1) & (n_far % 2 == 1))
    def _():
        tiles([(i - 2, None), (i - 1, 1), (i, 0)])

    lam_p = lam_ref[...].astype(F32)
    lam = (jnp.exp(jnp.sum(lam_p[0:1] * lam_p[1:2], axis=-1, keepdims=True))
           - jnp.exp(jnp.sum(lam_p[2:3] * lam_p[3:4], axis=-1, keepdims=True)) + lambda_init)
    o0 = acc_ref[0] * _rep(1.0 / l_ref[0], 2)
    o1 = acc_ref[1] * _rep(1.0 / l_ref[1], 2)
    att = o0 - lam * o1
    att = att * lax.rsqrt(jnp.mean(att * att, axis=-1, keepdims=True) + NORM_EPS)
    o_ref[...] = (att * gain_ref[...].astype(F32) * (1.0 - lambda_init)).astype(o_ref.dtype)


def _t5_bucket(dist):
    max_exact = REL_BUCKETS // 2
    d = jnp.maximum(dist, 1).astype(F32)
    large = max_exact + (jnp.log(d / max_exact) / math.log(REL_MAX_DIST / max_exact)
                         * (REL_BUCKETS - max_exact)).astype(jnp.int32)
    large = jnp.minimum(large, REL_BUCKETS - 1)
    return jnp.where(dist < max_exact, dist, large)


def _attn_bias_tiles(rel_bias, t):
    n = REL_MAX_DIST
    p = 2 * n
    nb = t // n
    table = rel_bias.astype(F32)[_t5_bucket(jnp.arange(p))]
    table = ((table - table[n][None]) * LOG2E).transpose(1, 2, 0)
    rot = jnp.tile(table, (1, 1, n + 1))[:, :, :n * (p + 1)].reshape(*table.shape[:2], n, p + 1)
    band0 = jnp.flip(rot[..., p - n + 1:p + 1], axis=-1)
    band1 = jnp.flip(rot[..., 1:n + 1], axis=-1)
    r = jnp.arange(t)[:, None]
    c = jnp.arange(t)[None, :]
    rb, cb = r // n, c // n
    band0, band1 = jnp.tile(band0, (1, 1, nb, nb)), jnp.tile(band1, (1, 1, nb, nb))
    diag = jnp.where(r < c, NEG_BIG,
                     jnp.where(rb == cb, band0, jnp.where(rb == cb + 1, band1, 0.0)))
    sub = jnp.where((rb == 0) & (cb == nb - 1), band1, 0.0)
    return jnp.stack([diag, sub], axis=2)


def _diff_attention(q, kv, bias, lam_params, gain, lambda_init, *, t):
    S, D = q.shape
    hw = 2 * HEAD_DIM
    H = D // hw
    return pl.pallas_call(
        functools.partial(_attn_kernel, t=t, lambda_init=lambda_init),
        grid=(H, S // t),
        in_specs=[pl.BlockSpec((t, hw), lambda h, i: (i, h)),
                  pl.BlockSpec((S, hw), lambda h, i: (0, h)),
                  pl.BlockSpec((S, hw), lambda h, i: (0, H + h)),
                  pl.BlockSpec((None, 2, 2, t, t), lambda h, i: (h, 0, 0, 0, 0)),
                  pl.BlockSpec((4, HEAD_DIM), lambda h, i: (0, 0)),
                  pl.BlockSpec((1, hw), lambda h, i: (0, 0))],
        out_specs=pl.BlockSpec((t, hw), lambda h, i: (i, h)),
        out_shape=jax.ShapeDtypeStruct((S, D), BF16),
        scratch_shapes=[pltpu.VMEM((2, t, 128), F32), pltpu.VMEM((2, t, 128), F32),
                        pltpu.VMEM((2, t, hw), F32)],
        compiler_params=_params("parallel", "arbitrary"), name="diff_attn",
    )(q, kv, kv, bias, lam_params, gain.reshape(1, hw))


def _router_kernel(h_ref, g_ref, w_ref, b_ref, xn_ref, ids_ref, wts_ref):
    x = h_ref[...]
    xn = x * lax.rsqrt(jnp.mean(x * x, axis=-1, keepdims=True) + NORM_EPS) * g_ref[...]
    xn_ref[...] = xn
    xh = xn.astype(BF16)
    xl = (xn - xh.astype(F32)).astype(BF16)
    w = w_ref[...]
    wh = w.astype(BF16)
    wl = (w - wh.astype(F32)).astype(BF16)
    logits = _dot(xh, wh) + _dot(xh, wl) + _dot(xl, wh) + b_ref[...]

    lane = lax.broadcasted_iota(jnp.int32, logits.shape, 1).astype(F32)
    first = lambda hit: jnp.min(jnp.where(hit, lane, 1e9), axis=-1, keepdims=True)
    gl = jnp.where(lane < MOE_GROUPS, logits, NEG_BIG)
    gmax = jnp.max(gl, axis=-1, keepdims=True)
    g_w = 1.0 / jnp.sum(jnp.exp(gl - gmax), axis=-1, keepdims=True)
    g_idx = first(gl == gmax)
    lo = MOE_GROUPS + MOE_PER_GROUP * g_idx
    el = jnp.where((lane >= lo) & (lane < lo + MOE_PER_GROUP), logits, NEG_BIG)
    e1 = jnp.max(el, axis=-1, keepdims=True)
    i1 = first(el == e1)
    el2 = jnp.where(lane == i1, NEG_BIG, el)
    e2 = jnp.max(el2, axis=-1, keepdims=True)
    i2 = first(el2 == e2)
    r = jnp.exp(e2 - e1)
    w1 = g_w / (1.0 + r)
    w2 = g_w * r / (1.0 + r)
    ids = jnp.where(lane == 0, i1, jnp.where(lane == 1, i2, float(MOE_GROUPS))) - MOE_GROUPS
    ids_ref[...] = ids.astype(jnp.int32)
    wts_ref[...] = jnp.where(lane == 0, w1, jnp.where(lane == 1, w2, 0.0))


def _router(h, gain, w_group, b_group, w_expert, b_expert, *, tr=256):
    S, D = h.shape
    n_used = MOE_GROUPS + MOE_EXPERTS
    w = jnp.pad(jnp.concatenate([w_group, w_expert], axis=1).astype(F32), ((0, 0), (0, 128 - n_used)))
    b = jnp.pad(jnp.concatenate([b_group.reshape(-1), b_expert.reshape(-1)]).astype(F32),
                (0, 128 - n_used)).reshape(1, 128)
    row = lambda width: pl.BlockSpec((tr, width), lambda i: (i, 0))
    full = lambda shape: pl.BlockSpec(shape, lambda i: (0, 0))
    xn, ids, wts = pl.pallas_call(
        _router_kernel, grid=(S // tr,),
        in_specs=[row(D), full((1, D)), full((D, 128)), full((1, 128))],
        out_specs=[row(D), row(128), row(128)],
        out_shape=[jax.ShapeDtypeStruct((S, D), F32), jax.ShapeDtypeStruct((S, 128), jnp.int32),
                   jax.ShapeDtypeStruct((S, 128), F32)],
        compiler_params=_params("parallel"), name="moe_router",
    )(h, gain.reshape(1, D), w, b)
    return xn, ids[:, :MOE_TOPK], wts[:, :MOE_TOPK]


def _dispatch(expert, weight, bm, n_tok):
    n_asg = n_tok * MOE_TOPK
    E = MOE_EXPERTS
    flat_e = expert.reshape(n_asg)
    onehot = (flat_e[:, None] == jnp.arange(E, dtype=jnp.int32)[None, :]).astype(jnp.int32)
    counts = jnp.sum(onehot, axis=0)
    rank = jnp.sum(jnp.cumsum(onehot, axis=0) * onehot, axis=1) - 1
    padded = (counts + bm - 1) // bm * bm
    pad_end = jnp.cumsum(padded)
    pad_start = pad_end - padded
    dest = jnp.sum(onehot * pad_start[None, :], axis=1) + rank
    n_blocks = -(-n_asg // bm) + E
    n_slots = n_blocks * bm
    n = jnp.arange(n_asg, dtype=jnp.int32)
    tok = n // MOE_TOPK
    row = (n % MOE_TOPK) * n_tok + tok
    w_bits = lax.bitcast_convert_type(weight.reshape(n_asg).astype(F32), jnp.int32)
    s = jnp.arange(n_slots, dtype=jnp.int32)
    spare = MOE_TOPK * n_tok + ((s // bm) % 2) * bm + s % bm
    empty = jnp.stack([jnp.zeros_like(s), spare, jnp.zeros_like(s)], axis=1)
    slots = empty.at[dest].set(jnp.stack([tok, row, w_bits], axis=1))
    block_start = jnp.arange(n_blocks, dtype=jnp.int32) * bm
    before = (pad_end[None, :] <= block_start[:, None]).astype(jnp.int32)
    block_e = jnp.minimum(jnp.sum(before, axis=1), E - 1)
    live = (block_start < pad_end[-1]).astype(jnp.int32)
    slot_w = lax.bitcast_convert_type(slots[:, 2], F32)
    return (block_e.astype(jnp.int32), live, slots[:, 0].reshape(n_blocks, 1, bm),
            slots[:, 1].reshape(n_blocks, 1, bm), slot_w.reshape(n_slots, 1), n_blocks)


def _expert_kernel(be_ref, live_ref, src_ref, nsrc_ref, pdst_ref, w_ref, x_hbm, wg_ref, wu_ref, wd_ref,
                   y_hbm, xbuf, ybuf, gsem, ssem, *, bm, n_rows):
    b = pl.program_id(0)
    live = live_ref[b] == 1
    prev_live = (b >= 1) & (live_ref[jnp.maximum(b - 1, 0)] == 1)
    ff = wd_ref.shape[0]
    lane = 128

    def gather_rows(idx_ref, buf_slot):
        def start(r):
            pltpu.make_async_copy(x_hbm.at[pl.ds(idx_ref[0, 0, r], 1), :],
                                  xbuf.at[buf_slot, pl.ds(r, 1), :], gsem.at[buf_slot]).start()
        return [functools.partial(start, r) for r in range(bm)]

    def scatter_rows(idx_ref, buf_slot):
        def start(r):
            pltpu.make_async_copy(ybuf.at[buf_slot, pl.ds(r, 1), :],
                                  y_hbm.at[pl.ds(idx_ref[0, 0, r], 1), :], ssem.at[buf_slot]).start()
        return [functools.partial(start, r) for r in range(bm)]

    def wait_block(sem, buf):
        pltpu.make_async_copy(buf, buf, sem).wait()

    def ffn(slot, starts):
        wg, wu = wg_ref[...], wu_ref[...]
        if ff % (2 * lane) == 0:
            w_cols = [wg, wu]
        else:
            w_cols = [wg[:, :ff - lane], jnp.concatenate([wg[:, ff - lane:], wu[:, :lane]], axis=1),
                      wu[:, lane:]]
        n_dots = EXPERT_ROW_PARTS * (len(w_cols) + 1)
        share = -(-len(starts) // n_dots)
        starts = list(starts)

        def dot_with_copies(a, w):
            for start in starts[:share]:
                start()
            del starts[:share]
            return _dot(a, w)

        part = bm // EXPERT_ROW_PARTS
        for i in range(EXPERT_ROW_PARTS):
            rows = pl.ds(i * part, part)
            xb = xbuf[slot, rows, :].astype(BF16)
            gu = jnp.concatenate([dot_with_copies(xb, w) for w in w_cols], axis=1)
            gate, up = gu[:, :ff], gu[:, ff:]
            hid = (gate * _sigmoid(gate)) * up
            ybuf[slot, rows, :] = dot_with_copies(hid.astype(BF16), wd_ref[...]) * w_ref[rows, :]
        assert not starts

    @pl.when(live & (b == 0))
    def _():
        ybuf[...] = jnp.zeros_like(ybuf)
        for p in range(2):
            spare = pltpu.make_async_copy(ybuf.at[p], y_hbm.at[pl.ds(n_rows + p * bm, bm), :], ssem.at[p])
            spare.start()
            spare.wait()
        for start in gather_rows(src_ref, 0):
            start()
        wait_block(gsem.at[0], xbuf.at[0])
        ffn(0, gather_rows(nsrc_ref, 1))

    for slot in range(2):
        @pl.when(live & (b >= 1) & (b % 2 == slot))
        def _(slot=slot):
            wait_block(gsem.at[slot], xbuf.at[slot])
            pl.when(b >= 2)(lambda: wait_block(ssem.at[slot], ybuf.at[slot]))
            ffn(slot, gather_rows(nsrc_ref, 1 - slot) + scatter_rows(pdst_ref, 1 - slot))

        @pl.when(jnp.logical_not(live) & prev_live & (b % 2 == slot))
        def _(slot=slot):
            wait_block(gsem.at[slot], xbuf.at[slot])
            pl.when(b >= 2)(lambda: wait_block(ssem.at[slot], ybuf.at[slot]))
            for start in scatter_rows(pdst_ref, 1 - slot):
                start()
            wait_block(ssem.at[1 - slot], ybuf.at[1 - slot])


def _moe_experts(xn, expert, weight, w_gate, w_up, w_down, layer, *, bm):
    S, D = xn.shape
    FF = w_down.shape[2]
    assert FF % 128 == 0
    be, live, src, dst, sw, n_blocks = _dispatch(expert, weight, bm, S)
    live = jnp.concatenate([live, jnp.zeros((1,), jnp.int32)])
    last = n_blocks - 1
    at = lambda off: (lambda b, be, live: (jnp.clip(b + off, 0, last), 0, 0))
    smem = lambda off: pl.BlockSpec((1, 1, bm), at(off), memory_space=pltpu.SMEM)
    wspec = lambda shape: pl.BlockSpec((None, None) + shape,
                                       lambda b, be, live: (layer, be[jnp.minimum(b, last)], 0, 0))
    return pl.pallas_call(
        functools.partial(_expert_kernel, bm=bm, n_rows=MOE_TOPK * S),
        grid_spec=pltpu.PrefetchScalarGridSpec(
            num_scalar_prefetch=2, grid=(n_blocks + 1,),
            in_specs=[smem(0), smem(1), smem(-1),
                      pl.BlockSpec((bm, 1), lambda b, be, live: (jnp.minimum(b, last), 0)),
                      pl.BlockSpec(memory_space=pl.ANY),
                      wspec((D, FF)), wspec((D, FF)), wspec((FF, D))],
            out_specs=pl.BlockSpec(memory_space=pl.ANY),
            scratch_shapes=[pltpu.VMEM((2, bm, D), F32), pltpu.VMEM((2, bm, D), F32),
                            pltpu.SemaphoreType.DMA((2,)), pltpu.SemaphoreType.DMA((2,))]),
        out_shape=jax.ShapeDtypeStruct((MOE_TOPK * S + 2 * bm, D), F32),
        compiler_params=_params("arbitrary"), name="moe_experts",
    )(be, live, src, src, dst, sw, xn, w_gate, w_up, w_down)


def _moe(h, gain, w_group, b_group, w_expert, b_expert, w_gate, w_up, w_down, layer, *, bm):
    xn, expert, weight = _router(h, gain, w_group, b_group, w_expert, b_expert)
    return _moe_experts(xn, expert, weight, w_gate, w_up, w_down, layer, bm=bm)


def _forward(x, ln_mix, ln_ffn, hg_w_in, hg_lower_bound, hg_out_norm, hg_w_o, kv_norm, w_kv,
             da_w_q, da_lambda_q1, da_lambda_k1, da_lambda_q2, da_lambda_k2, da_subln, da_w_o,
             rel_bias, moe_w_group, moe_b_group, moe_w_expert, moe_b_expert, moe_w_gate,
             moe_w_up, moe_w_down, final_norm, *, attn_tile, moe_block):
    B, S, D = x.shape
    assert B == 1 and ln_mix.shape[0] == 2 and hg_w_in.shape[0] == 1 and da_w_q.shape[0] == 1
    h = x.reshape(S, D)

    (xn,) = _norm(h, ln_mix[0:1], [BF16])
    proj = _matmul(xn, hg_w_in[0].astype(BF16), out_dtype=F32, name="hg_in_proj")
    mix = _hgrn2(proj, hg_lower_bound, hg_out_norm[0])
    h = _matmul(mix, hg_w_o[0].astype(BF16), out_dtype=F32, res=h, name="hg_out_proj")
    wg, wu, wd = moe_w_gate.astype(BF16), moe_w_up.astype(BF16), moe_w_down.astype(BF16)
    y = _moe(h, ln_ffn[0], moe_w_group[0], moe_b_group[0], moe_w_expert[0], moe_b_expert[0],
             wg, wu, wd, 0, bm=moe_block)

    layer = 1
    lambda_init = 0.8 - 0.6 * math.exp(-0.3 * layer)
    h, xkv, xq = _norm(h, jnp.stack([kv_norm, ln_mix[1]]), [BF16, BF16], y=y, emit_sum=True)
    kv = _matmul(xkv, w_kv.astype(BF16), out_dtype=BF16, name="kv_proj")
    q = _matmul(xq, da_w_q[0].astype(BF16), out_dtype=BF16, scale=HEAD_DIM ** -0.5 * LOG2E,
                name="q_proj")
    lam_params = jnp.stack([da_lambda_q1[0], da_lambda_k1[0], da_lambda_q2[0], da_lambda_k2[0]])
    att = _diff_attention(q, kv, _attn_bias_tiles(rel_bias, attn_tile), lam_params, da_subln[0],
                          lambda_init, t=attn_tile)
    h = _matmul(att, da_w_o[0].astype(BF16), out_dtype=F32, res=h, name="da_out_proj")
    y = _moe(h, ln_ffn[1], moe_w_group[1], moe_b_group[1], moe_w_expert[1], moe_b_expert[1],
             wg, wu, wd, 1, bm=moe_block)

    (out,) = _norm(h, final_norm.reshape(1, D), [F32], y=y)
    return out.reshape(B, S, D)


def kernel(x, ln_mix, ln_ffn, hg_w_in, hg_lower_bound, hg_out_norm, hg_w_o, kv_norm, w_kv, da_w_q, da_lambda_q1, da_lambda_k1, da_lambda_q2, da_lambda_k2, da_subln, da_w_o, rel_bias, moe_w_group, moe_b_group, moe_w_expert, moe_b_expert, moe_w_gate, moe_w_up, moe_w_down, final_norm):
    return _forward(x, ln_mix, ln_ffn, hg_w_in, hg_lower_bound, hg_out_norm, hg_w_o, kv_norm, w_kv,
                    da_w_q, da_lambda_q1, da_lambda_k1, da_lambda_q2, da_lambda_k2, da_subln,
                    da_w_o, rel_bias, moe_w_group, moe_b_group, moe_w_expert, moe_b_expert,
                    moe_w_gate, moe_w_up, moe_w_down, final_norm,
                    attn_tile=min(512, x.shape[1]), moe_block=256)
```

```python
import functools
import math

import numpy as np
import jax
import jax.numpy as jnp
from jax import lax
from jax.experimental import pallas as pl
from jax.experimental.pallas import tpu as pltpu

F32 = jnp.float32
BF16 = jnp.bfloat16

NORM_EPS = 1e-6
HEAD_DIM = 128
HG_CHUNK = 64
HG_TILE = 256
HG_HEADS_PER_STEP = 4
MOE_GROUPS = 8
MOE_PER_GROUP = 8
MOE_EXPERTS = MOE_GROUPS * MOE_PER_GROUP
MOE_TOPK = 2
EXPERT_ROW_PARTS = 2
REL_BUCKETS = 32
REL_MAX_DIST = 128
NEG_BIG = -1e30
LOG2E = 1.4426950408889634
VMEM_LIMIT_BYTES = 56 * 1024 * 1024


def _params(*sem):
    return pltpu.CompilerParams(dimension_semantics=sem, vmem_limit_bytes=VMEM_LIMIT_BYTES)


def _sigmoid(x):
    return 1.0 / (1.0 + jnp.exp(-x))


def _dot(a, b):
    return jnp.dot(a, b, preferred_element_type=F32)


def _dot_nt(a, b):
    return lax.dot_general(a, b, (((1,), (1,)), ((), ())), preferred_element_type=F32)


def _dot_tn(a, b):
    return lax.dot_general(a, b, (((0,), (0,)), ((), ())), preferred_element_type=F32)


_HIGH_HALF = 0xFFFF0000


def _pack_halves(x):
    d = x.shape[1] // 2
    bits = lambda v: lax.bitcast_convert_type(v.astype(BF16).astype(F32), jnp.uint32)
    return (bits(x[:, :d]) >> 16) | (bits(x[:, d:]) & jnp.uint32(_HIGH_HALF))


def _unpack_halves(w):
    lo = lax.bitcast_convert_type(w << 16, F32)
    hi = lax.bitcast_convert_type(w & jnp.uint32(_HIGH_HALF), F32)
    return jnp.concatenate([lo, hi], axis=1)


def _norm_kernel(*refs, n_in, n_gain, emit_sum):
    in_refs = refs[:n_in]
    g_ref = refs[n_in]
    out_refs = refs[n_in + 1:]
    x = in_refs[0][...].astype(F32)
    for r in in_refs[1:]:
        x = x + _unpack_halves(r[...])
    k = 0
    if emit_sum:
        out_refs[0][...] = x
        k = 1
    xn = x * lax.rsqrt(jnp.mean(x * x, axis=-1, keepdims=True) + NORM_EPS)
    for j in range(n_gain):
        out_refs[k + j][...] = (xn * g_ref[j:j + 1, :]).astype(out_refs[k + j].dtype)


def _norm(h, gains, out_dtypes, *, y=None, emit_sum=False, tr=256):
    S, D = h.shape
    n_gain = gains.shape[0]
    ins = [h]
    in_specs = [pl.BlockSpec((tr, D), lambda i: (i, 0))]
    if y is not None:
        ins += [y, y]
        in_specs += [pl.BlockSpec((tr, D // 2), lambda i: (i, 0)),
                     pl.BlockSpec((tr, D // 2), lambda i: (S // tr + i, 0))]
    ins.append(gains)
    in_specs.append(pl.BlockSpec((n_gain, D), lambda i: (0, 0)))
    out_shape = []
    if emit_sum:
        out_shape.append(jax.ShapeDtypeStruct((S, D), F32))
    out_shape += [jax.ShapeDtypeStruct((S, D), dt) for dt in out_dtypes]
    out_specs = [pl.BlockSpec((tr, D), lambda i: (i, 0)) for _ in out_shape]
    return pl.pallas_call(
        functools.partial(_norm_kernel, n_in=len(ins) - 1, n_gain=n_gain, emit_sum=emit_sum),
        grid=(S // tr,), in_specs=in_specs, out_specs=out_specs, out_shape=out_shape,
        compiler_params=_params("parallel"), name="norm",
    )(*ins)


def _mm_kernel(*refs, has_res, scale):
    a_ref, w_ref = refs[0], refs[1]
    o_ref = refs[-1]
    acc = _dot(a_ref[...], w_ref[...])
    if scale is not None:
        acc = acc * scale
    if has_res:
        acc = acc + refs[2][...]
    o_ref[...] = acc.astype(o_ref.dtype)


def _matmul(a, w, *, out_dtype, res=None, scale=None, tm=1024, tn=512, name="matmul"):
    M, K = a.shape
    N = w.shape[1]
    tm, tn = min(tm, M), min(tn, N)
    ins = [a, w]
    in_specs = [pl.BlockSpec((tm, K), lambda i, j: (i, 0)),
                pl.BlockSpec((K, tn), lambda i, j: (0, j))]
    if res is not None:
        ins.append(res)
        in_specs.append(pl.BlockSpec((tm, tn), lambda i, j: (i, j)))
    return pl.pallas_call(
        functools.partial(_mm_kernel, has_res=res is not None, scale=scale),
        grid=(M // tm, N // tn), in_specs=in_specs,
        out_specs=pl.BlockSpec((tm, tn), lambda i, j: (i, j)),
        out_shape=jax.ShapeDtypeStruct((M, N), out_dtype),
        compiler_params=_params("parallel", "arbitrary"), name=name,
    )(*ins)


_HG_LEVELS = (32, 16, 8, 4, 2, 1)


def _hgrn_decay_matrix():
    C = HG_CHUNK
    t = np.arange(C)[:, None]
    j = np.arange(C)[None, :]
    blocks = [(j <= t), (j > t)]
    for w in _HG_LEVELS:
        r = (t // (2 * w)) * (2 * w) + w - 1
        upper = (t // w) % 2 == 1
        blocks.append(np.where(upper, (j > r) & (j <= t), (j > t) & (j <= r)))
    return np.concatenate(blocks, axis=0).astype(np.float32)


def _hgrn_kernel(q_ref, f_ref, v_ref, g_ref, lb_ref, gain_ref, m_ref, o_ref, st_ref, *, heads):
    C = HG_CHUNK
    dk = HEAD_DIM

    @pl.when(pl.program_id(1) == 0)
    def _():
        st_ref[...] = jnp.zeros_like(st_ref)

    lbr = lb_ref[...].astype(F32)
    lbe = jnp.exp(lbr - jnp.max(lbr, axis=0, keepdims=True))
    lb_all = lbe[0:1, :] / jnp.sum(lbe, axis=0, keepdims=True)
    gain = gain_ref[...].astype(F32)
    dec = m_ref[...]

    row = lax.broadcasted_iota(jnp.int32, (C, dk), 0)
    ti = lax.broadcasted_iota(jnp.int32, (C, C), 0)
    si = lax.broadcasted_iota(jnp.int32, (C, C), 1)
    upper = [((row // w) % 2) == 1 for w in _HG_LEVELS]
    same = [(ti // (2 * w)) == (si // (2 * w)) for w in _HG_LEVELS]

    units = [(slice(c * C, (c + 1) * C), hh, slice(hh * dk, (hh + 1) * dk))
             for c in range(q_ref.shape[0] // C) for hh in range(heads)]
    qf, kk, v, sums = [], [], [], []
    for sl, hh, cols in units:
        lb = lb_all[:, cols]
        q = q_ref[sl, cols].astype(F32)
        qf.append(q * _sigmoid(q))
        fg = lb + (1.0 - lb) * _sigmoid(f_ref[sl, cols].astype(F32))
        logf = jnp.log(fg)
        kk.append(1.0 - fg)
        v.append(v_ref[sl, cols].astype(BF16))
        hi = logf.astype(BF16)
        lo = (logf - hi.astype(F32)).astype(BF16)
        sums2 = _dot(dec, jnp.concatenate([hi, lo], axis=1))
        sums.append(sums2[:, :dk] + sums2[:, dk:])

    a = [jnp.where(ti == si, jnp.sum(qf[u] * kk[u], axis=-1, keepdims=True), 0.0)
         for u in range(len(units))]
    for l in range(len(_HG_LEVELS)):
        for u in range(len(units)):
            e = jnp.exp(sums[u][(2 + l) * C:(3 + l) * C])
            qw = jnp.where(upper[l], qf[u] * e, 0.0).astype(BF16)
            kw = jnp.where(upper[l], 0.0, kk[u] * e).astype(BF16)
            a[u] = a[u] + jnp.where(same[l], _dot_nt(qw, kw), 0.0)

    intra = [_dot(a[u].astype(BF16), v[u]) for u in range(len(units))]
    qb = [(qf[u] * jnp.exp(sums[u][0:C])).astype(BF16) for u in range(len(units))]
    kd = [(kk[u] * jnp.exp(sums[u][C:2 * C])).astype(BF16) for u in range(len(units))]
    st = [st_ref[hh] for hh in range(heads)]
    for u, (sl, hh, cols) in enumerate(units):
        o = intra[u] + _dot_nt(qb[u], st[hh].astype(BF16))
        st[hh] = st[hh] * jnp.exp(sums[u][C - 1:C, :]) + _dot_tn(v[u], kd[u])
        g = g_ref[sl, cols].astype(F32)
        on = o * lax.rsqrt(jnp.mean(o * o, axis=-1, keepdims=True) + NORM_EPS) * gain
        o_ref[sl, cols] = (on * (g * _sigmoid(g))).astype(o_ref.dtype)
    for hh in range(heads):
        st_ref[hh] = st[hh]


def _hgrn2(proj, lower_bound, out_gain):
    S = proj.shape[0]
    D = proj.shape[1] // 4
    H = D // HEAD_DIM
    T = min(HG_TILE, S)
    hpb = min(HG_HEADS_PER_STEP, H)
    G = H // hpb
    w = hpb * HEAD_DIM
    dec = jnp.asarray(_hgrn_decay_matrix(), BF16)
    blk = lambda off: pl.BlockSpec((T, w), lambda h, i, off=off: (i, off + h))
    return pl.pallas_call(
        functools.partial(_hgrn_kernel, heads=hpb),
        grid=(G, S // T),
        in_specs=[blk(0), blk(G), blk(2 * G), blk(3 * G),
                  pl.BlockSpec((lower_bound.shape[0], w), lambda h, i: (0, h)),
                  pl.BlockSpec((1, HEAD_DIM), lambda h, i: (0, 0)),
                  pl.BlockSpec(dec.shape, lambda h, i: (0, 0))],
        out_specs=pl.BlockSpec((T, w), lambda h, i: (i, h)),
        out_shape=jax.ShapeDtypeStruct((S, D), BF16),
        scratch_shapes=[pltpu.VMEM((hpb, HEAD_DIM, HEAD_DIM), F32)],
        compiler_params=_params("parallel", "arbitrary"), name="hgrn2",
    )(proj, proj, proj, proj, lower_bound, out_gain.reshape(1, HEAD_DIM), dec)


def _rep(x, n):
    return x if n == 1 else jnp.concatenate([x] * n, axis=1)


def _attn_kernel(q_ref, k_ref, v_ref, bias_ref, lam_ref, gain_ref, o_ref, m_ref, l_ref, acc_ref,
                 *, t, lambda_init):
    i = pl.program_id(1)
    dh = HEAD_DIM
    maps = range(2)
    m_ref[...] = jnp.full_like(m_ref, NEG_BIG)
    l_ref[...] = jnp.zeros_like(l_ref)
    acc_ref[...] = jnp.zeros_like(acc_ref)

    def scores(j, width):
        r0 = pl.multiple_of(j * t, t)
        return [_dot_nt(q_ref[:, m * dh:(m + 1) * dh], k_ref[pl.ds(r0, width), m * dh:(m + 1) * dh])
                for m in maps]

    def absorb(j, width, s):
        r0 = pl.multiple_of(j * t, t)
        vt = v_ref[pl.ds(r0, width), :]
        m_prev = [m_ref[m] for m in maps]
        m_next = [jnp.maximum(m_prev[m], jnp.max(s(m), axis=-1, keepdims=True)) for m in maps]
        p = [jnp.exp2(s(m) - _rep(m_next[m], width // 128)) for m in maps]
        alpha = [jnp.exp2(m_prev[m] - m_next[m]) for m in maps]
        pv = [_dot(p[m].astype(BF16), vt) for m in maps]
        for m in maps:
            l_ref[m] = alpha[m] * l_ref[m] + jnp.sum(p[m], axis=-1, keepdims=True)
            acc_ref[m] = acc_ref[m] * _rep(alpha[m], 2) + pv[m]
            m_ref[m] = m_next[m]

    def span(j, width, bias):
        s = scores(j, width)
        if bias is not None:
            s = [s[m] + bias(m) for m in maps]
        absorb(j, width, lambda m: s[m])

    n_far = jnp.maximum(i - 1, 0)

    def far_pair(jj, carry):
        span(2 * jj, 2 * t, None)
        return carry

    lax.fori_loop(0, n_far // 2, far_pair, 0)

    @pl.when(n_far % 2 == 1)
    def _():
        span(i - 2, t, None)

    @pl.when(i == 0)
    def _():
        span(i, t, lambda m: bias_ref[m, :, t:2 * t])

    @pl.when(i >= 1)
    def _():
        span(i - 1, 2 * t, lambda m: bias_ref[m])

    lam_p = lam_ref[...].astype(F32)
    lam = (jnp.exp(jnp.sum(lam_p[0:1] * lam_p[1:2], axis=-1, keepdims=True))
           - jnp.exp(jnp.sum(lam_p[2:3] * lam_p[3:4], axis=-1, keepdims=True)) + lambda_init)
    o0 = acc_ref[0] * _rep(1.0 / l_ref[0], 2)
    o1 = acc_ref[1] * _rep(1.0 / l_ref[1], 2)
    att = o0 - lam * o1
    att = att * lax.rsqrt(jnp.mean(att * att, axis=-1, keepdims=True) + NORM_EPS)
    o_ref[...] = (att * gain_ref[...].astype(F32) * (1.0 - lambda_init)).astype(o_ref.dtype)


def _t5_bucket(dist):
    max_exact = REL_BUCKETS // 2
    d = jnp.maximum(dist, 1).astype(F32)
    large = max_exact + (jnp.log(d / max_exact) / math.log(REL_MAX_DIST / max_exact)
                         * (REL_BUCKETS - max_exact)).astype(jnp.int32)
    large = jnp.minimum(large, REL_BUCKETS - 1)
    return jnp.where(dist < max_exact, dist, large)


def _attn_bias_tiles(rel_bias, t):
    n = REL_MAX_DIST
    p = 2 * n
    nb = t // n
    table = rel_bias.astype(F32)[_t5_bucket(jnp.arange(p))]
    table = ((table - table[n][None]) * LOG2E).transpose(1, 2, 0)
    rot = jnp.tile(table, (1, 1, n + 1))[:, :, :n * (p + 1)].reshape(*table.shape[:2], n, p + 1)
    band0 = jnp.flip(rot[..., p - n + 1:p + 1], axis=-1)
    band1 = jnp.flip(rot[..., 1:n + 1], axis=-1)
    r = jnp.arange(t)[:, None]
    c = jnp.arange(t)[None, :]
    rb, cb = r // n, c // n
    band0, band1 = jnp.tile(band0, (1, 1, nb, nb)), jnp.tile(band1, (1, 1, nb, nb))
    diag = jnp.where(r < c, NEG_BIG,
                     jnp.where(rb == cb, band0, jnp.where(rb == cb + 1, band1, 0.0)))
    sub = jnp.where((rb == 0) & (cb == nb - 1), band1, 0.0)
    return jnp.concatenate([sub, diag], axis=-1)


def _diff_attention(q, kv, bias, lam_params, gain, lambda_init, *, t):
    S, D = q.shape
    hw = 2 * HEAD_DIM
    H = D // hw
    per_head = dict(pipeline_mode=pl.Buffered(1))
    return pl.pallas_call(
        functools.partial(_attn_kernel, t=t, lambda_init=lambda_init),
        grid=(H, S // t),
        in_specs=[pl.BlockSpec((t, hw), lambda h, i: (i, h)),
                  pl.BlockSpec((S, hw), lambda h, i: (0, h), **per_head),
                  pl.BlockSpec((S, hw), lambda h, i: (0, H + h), **per_head),
                  pl.BlockSpec((None, 2, t, 2 * t), lambda h, i: (h, 0, 0, 0), **per_head),
                  pl.BlockSpec((4, HEAD_DIM), lambda h, i: (0, 0)),
                  pl.BlockSpec((1, hw), lambda h, i: (0, 0))],
        out_specs=pl.BlockSpec((t, hw), lambda h, i: (i, h)),
        out_shape=jax.ShapeDtypeStruct((S, D), BF16),
        scratch_shapes=[pltpu.VMEM((2, t, 128), F32), pltpu.VMEM((2, t, 128), F32),
                        pltpu.VMEM((2, t, hw), F32)],
        compiler_params=_params("parallel", "arbitrary"), name="diff_attn",
    )(q, kv, kv, bias, lam_params, gain.reshape(1, hw))


def _router_kernel(h_ref, g_ref, w_ref, b_ref, xn_ref, ids_ref, wts_ref):
    x = h_ref[...]
    xn = x * lax.rsqrt(jnp.mean(x * x, axis=-1, keepdims=True) + NORM_EPS) * g_ref[...]
    xn_ref[...] = _pack_halves(xn)
    xh = xn.astype(BF16)
    xl = (xn - xh.astype(F32)).astype(BF16)
    w = w_ref[...]
    wh = w.astype(BF16)
    wl = (w - wh.astype(F32)).astype(BF16)
    logits = _dot(xh, wh) + _dot(xh, wl) + _dot(xl, wh) + b_ref[...]

    lane = lax.broadcasted_iota(jnp.int32, logits.shape, 1).astype(F32)
    first = lambda hit: jnp.min(jnp.where(hit, lane, 1e9), axis=-1, keepdims=True)
    gl = jnp.where(lane < MOE_GROUPS, logits, NEG_BIG)
    gmax = jnp.max(gl, axis=-1, keepdims=True)
    g_w = 1.0 / jnp.sum(jnp.exp(gl - gmax), axis=-1, keepdims=True)
    g_idx = first(gl == gmax)
    lo = MOE_GROUPS + MOE_PER_GROUP * g_idx
    el = jnp.where((lane >= lo) & (lane < lo + MOE_PER_GROUP), logits, NEG_BIG)
    e1 = jnp.max(el, axis=-1, keepdims=True)
    i1 = first(el == e1)
    el2 = jnp.where(lane == i1, NEG_BIG, el)
    e2 = jnp.max(el2, axis=-1, keepdims=True)
    i2 = first(el2 == e2)
    r = jnp.exp(e2 - e1)
    w1 = g_w / (1.0 + r)
    w2 = g_w * r / (1.0 + r)
    ids = jnp.where(lane == 0, i1, jnp.where(lane == 1, i2, float(MOE_GROUPS))) - MOE_GROUPS
    ids_ref[...] = ids.astype(jnp.int32)
    wts_ref[...] = jnp.where(lane == 0, w1, jnp.where(lane == 1, w2, 0.0))


def _router(h, gain, w_group, b_group, w_expert, b_expert, *, tr=256):
    S, D = h.shape
    n_used = MOE_GROUPS + MOE_EXPERTS
    w = jnp.pad(jnp.concatenate([w_group, w_expert], axis=1).astype(F32), ((0, 0), (0, 128 - n_used)))
    b = jnp.pad(jnp.concatenate([b_group.reshape(-1), b_expert.reshape(-1)]).astype(F32),
                (0, 128 - n_used)).reshape(1, 128)
    row = lambda width: pl.BlockSpec((tr, width), lambda i: (i, 0))
    full = lambda shape: pl.BlockSpec(shape, lambda i: (0, 0))
    xn, ids, wts = pl.pallas_call(
        _router_kernel, grid=(S // tr,),
        in_specs=[row(D), full((1, D)), full((D, 128)), full((1, 128))],
        out_specs=[row(D // 2), row(128), row(128)],
        out_shape=[jax.ShapeDtypeStruct((S, D // 2), jnp.uint32), jax.ShapeDtypeStruct((S, 128), jnp.int32),
                   jax.ShapeDtypeStruct((S, 128), F32)],
        compiler_params=_params("parallel"), name="moe_router",
    )(h, gain.reshape(1, D), w, b)
    return xn, ids[:, :MOE_TOPK], wts[:, :MOE_TOPK]


def _dispatch(expert, weight, bm, n_tok):
    n_asg = n_tok * MOE_TOPK
    E = MOE_EXPERTS
    flat_e = expert.reshape(n_asg)
    onehot = (flat_e[:, None] == jnp.arange(E, dtype=jnp.int32)[None, :]).astype(jnp.int32)
    counts = jnp.sum(onehot, axis=0)
    rank = jnp.sum(jnp.cumsum(onehot, axis=0) * onehot, axis=1) - 1
    padded = (counts + bm - 1) // bm * bm
    pad_end = jnp.cumsum(padded)
    pad_start = pad_end - padded
    dest = jnp.sum(onehot * pad_start[None, :], axis=1) + rank
    n_blocks = -(-n_asg // bm) + E
    n_slots = n_blocks * bm
    n = jnp.arange(n_asg, dtype=jnp.int32)
    tok = n // MOE_TOPK
    row = (n % MOE_TOPK) * n_tok + tok
    w_bits = lax.bitcast_convert_type(weight.reshape(n_asg).astype(F32), jnp.int32)
    s = jnp.arange(n_slots, dtype=jnp.int32)
    spare = MOE_TOPK * n_tok + ((s // bm) % 2) * bm + s % bm
    empty = jnp.stack([jnp.zeros_like(s), spare, jnp.zeros_like(s)], axis=1)
    slots = empty.at[dest].set(jnp.stack([tok, row, w_bits], axis=1))
    block_start = jnp.arange(n_blocks, dtype=jnp.int32) * bm
    before = (pad_end[None, :] <= block_start[:, None]).astype(jnp.int32)
    block_e = jnp.minimum(jnp.sum(before, axis=1), E - 1)
    live = (block_start < pad_end[-1]).astype(jnp.int32)
    slot_w = lax.bitcast_convert_type(slots[:, 2], F32)
    return (block_e.astype(jnp.int32), live, slots[:, 0].reshape(n_blocks, 1, bm),
            slots[:, 1].reshape(n_blocks, 1, bm), slot_w.reshape(n_slots, 1), n_blocks)


def _expert_kernel(be_ref, live_ref, src_ref, nsrc_ref, pdst_ref, w_ref, x_hbm, wg_ref, wu_ref, wd_ref,
                   y_hbm, xbuf, ybuf, gsem, ssem, *, bm, n_rows):
    b = pl.program_id(0)
    live = live_ref[b] == 1
    prev_live = (b >= 1) & (live_ref[jnp.maximum(b - 1, 0)] == 1)
    ff = wd_ref.shape[0]
    lane = 128

    def gather_rows(idx_ref, buf_slot):
        def start(r):
            pltpu.make_async_copy(x_hbm.at[pl.ds(idx_ref[0, 0, r], 1), :],
                                  xbuf.at[buf_slot, pl.ds(r, 1), :], gsem.at[buf_slot]).start()
        return [functools.partial(start, r) for r in range(bm)]

    def scatter_rows(idx_ref, buf_slot):
        def start(r):
            pltpu.make_async_copy(ybuf.at[buf_slot, pl.ds(r, 1), :],
                                  y_hbm.at[pl.ds(idx_ref[0, 0, r], 1), :], ssem.at[buf_slot]).start()
        return [functools.partial(start, r) for r in range(bm)]

    def wait_block(sem, buf):
        pltpu.make_async_copy(buf, buf, sem).wait()

    def ffn(slot, starts):
        wg, wu = wg_ref[...], wu_ref[...]
        if ff % (2 * lane) == 0:
            w_cols = [wg, wu]
        else:
            w_cols = [wg[:, :ff - lane], jnp.concatenate([wg[:, ff - lane:], wu[:, :lane]], axis=1),
                      wu[:, lane:]]
        n_dots = EXPERT_ROW_PARTS * (len(w_cols) + 1)
        share = -(-len(starts) // n_dots)
        starts = list(starts)

        def dot_with_copies(a, w):
            for start in starts[:share]:
                start()
            del starts[:share]
            return _dot(a, w)

        part = bm // EXPERT_ROW_PARTS
        for i in range(EXPERT_ROW_PARTS):
            rows = pl.ds(i * part, part)
            xb = _unpack_halves(xbuf[slot, rows, :]).astype(BF16)
            gu = jnp.concatenate([dot_with_copies(xb, w) for w in w_cols], axis=1)
            gate, up = gu[:, :ff], gu[:, ff:]
            hid = (gate * _sigmoid(gate)) * up
            ybuf[slot, rows, :] = _pack_halves(dot_with_copies(hid.astype(BF16), wd_ref[...])
                                               * w_ref[rows, :])
        assert not starts

    @pl.when(live & (b == 0))
    def _():
        ybuf[...] = jnp.zeros_like(ybuf)
        for p in range(2):
            spare = pltpu.make_async_copy(ybuf.at[p], y_hbm.at[pl.ds(n_rows + p * bm, bm), :], ssem.at[p])
            spare.start()
            spare.wait()
        for start in gather_rows(src_ref, 0):
            start()
        wait_block(gsem.at[0], xbuf.at[0])
        ffn(0, gather_rows(nsrc_ref, 1))

    for slot in range(2):
        @pl.when(live & (b >= 1) & (b % 2 == slot))
        def _(slot=slot):
            wait_block(gsem.at[slot], xbuf.at[slot])
            pl.when(b >= 2)(lambda: wait_block(ssem.at[slot], ybuf.at[slot]))
            ffn(slot, gather_rows(nsrc_ref, 1 - slot) + scatter_rows(pdst_ref, 1 - slot))

        @pl.when(jnp.logical_not(live) & prev_live & (b % 2 == slot))
        def _(slot=slot):
            wait_block(gsem.at[slot], xbuf.at[slot])
            pl.when(b >= 2)(lambda: wait_block(ssem.at[slot], ybuf.at[slot]))
            for start in scatter_rows(pdst_ref, 1 - slot):
                start()
            wait_block(ssem.at[1 - slot], ybuf.at[1 - slot])


def _moe_experts(xn, expert, weight, w_gate, w_up, w_down, layer, *, bm):
    S, half = xn.shape
    D = 2 * half
    FF = w_down.shape[2]
    assert FF % 128 == 0
    be, live, src, dst, sw, n_blocks = _dispatch(expert, weight, bm, S)
    live = jnp.concatenate([live, jnp.zeros((1,), jnp.int32)])
    last = n_blocks - 1
    at = lambda off: (lambda b, be, live: (jnp.clip(b + off, 0, last), 0, 0))
    smem = lambda off: pl.BlockSpec((1, 1, bm), at(off), memory_space=pltpu.SMEM)
    wspec = lambda shape: pl.BlockSpec((None, None) + shape,
                                       lambda b, be, live: (layer, be[jnp.minimum(b, last)], 0, 0))
    return pl.pallas_call(
        functools.partial(_expert_kernel, bm=bm, n_rows=MOE_TOPK * S),
        grid_spec=pltpu.PrefetchScalarGridSpec(
            num_scalar_prefetch=2, grid=(n_blocks + 1,),
            in_specs=[smem(0), smem(1), smem(-1),
                      pl.BlockSpec((bm, 1), lambda b, be, live: (jnp.minimum(b, last), 0)),
                      pl.BlockSpec(memory_space=pl.ANY),
                      wspec((D, FF)), wspec((D, FF)), wspec((FF, D))],
            out_specs=pl.BlockSpec(memory_space=pl.ANY),
            scratch_shapes=[pltpu.VMEM((2, bm, half), jnp.uint32), pltpu.VMEM((2, bm, half), jnp.uint32),
                            pltpu.SemaphoreType.DMA((2,)), pltpu.SemaphoreType.DMA((2,))]),
        out_shape=jax.ShapeDtypeStruct((MOE_TOPK * S + 2 * bm, half), jnp.uint32),
        compiler_params=_params("arbitrary"), name="moe_experts",
    )(be, live, src, src, dst, sw, xn, w_gate, w_up, w_down)


def _moe(h, gain, w_group, b_group, w_expert, b_expert, w_gate, w_up, w_down, layer, *, bm):
    xn, expert, weight = _router(h, gain, w_group, b_group, w_expert, b_expert)
    return _moe_experts(xn, expert, weight, w_gate, w_up, w_down, layer, bm=bm)


def _forward(x, ln_mix, ln_ffn, hg_w_in, hg_lower_bound, hg_out_norm, hg_w_o, kv_norm, w_kv,
             da_w_q, da_lambda_q1, da_lambda_k1, da_lambda_q2, da_lambda_k2, da_subln, da_w_o,
             rel_bias, moe_w_group, moe_b_group, moe_w_expert, moe_b_expert, moe_w_gate,
             moe_w_up, moe_w_down, final_norm, *, attn_tile, moe_block):
    B, S, D = x.shape
    assert B == 1 and ln_mix.shape[0] == 2 and hg_w_in.shape[0] == 1 and da_w_q.shape[0] == 1
    h = x.reshape(S, D)

    (xn,) = _norm(h, ln_mix[0:1], [BF16])
    proj = _matmul(xn, hg_w_in[0].astype(BF16), out_dtype=BF16, name="hg_in_proj")
    mix = _hgrn2(proj, hg_lower_bound, hg_out_norm[0])
    h = _matmul(mix, hg_w_o[0].astype(BF16), out_dtype=F32, res=h, name="hg_out_proj")
    wg, wu, wd = moe_w_gate.astype(BF16), moe_w_up.astype(BF16), moe_w_down.astype(BF16)
    y = _moe(h, ln_ffn[0], moe_w_group[0], moe_b_group[0], moe_w_expert[0], moe_b_expert[0],
             wg, wu, wd, 0, bm=moe_block)

    layer = 1
    lambda_init = 0.8 - 0.6 * math.exp(-0.3 * layer)
    h, xkv, xq = _norm(h, jnp.stack([kv_norm, ln_mix[1]]), [BF16, BF16], y=y, emit_sum=True)
    kv = _matmul(xkv, w_kv.astype(BF16), out_dtype=BF16, name="kv_proj")
    q = _matmul(xq, da_w_q[0].astype(BF16), out_dtype=BF16, scale=HEAD_DIM ** -0.5 * LOG2E,
                name="q_proj")
    lam_params = jnp.stack([da_lambda_q1[0], da_lambda_k1[0], da_lambda_q2[0], da_lambda_k2[0]])
    att = _diff_attention(q, kv, _attn_bias_tiles(rel_bias, attn_tile), lam_params, da_subln[0],
                          lambda_init, t=attn_tile)
    h = _matmul(att, da_w_o[0].astype(BF16), out_dtype=F32, res=h, name="da_out_proj")
    y = _moe(h, ln_ffn[1], moe_w_group[1], moe_b_group[1], moe_w_expert[1], moe_b_expert[1],
             wg, wu, wd, 1, bm=moe_block)

    (out,) = _norm(h, final_norm.reshape(1, D), [F32], y=y)
    return out.reshape(B, S, D)


def kernel(x, ln_mix, ln_ffn, hg_w_in, hg_lower_bound, hg_out_norm, hg_w_o, kv_norm, w_kv, da_w_q, da_lambda_q1, da_lambda_k1, da_lambda_q2, da_lambda_k2, da_subln, da_w_o, rel_bias, moe_w_group, moe_b_group, moe_w_expert, moe_b_expert, moe_w_gate, moe_w_up, moe_w_down, final_norm):
    return _forward(x, ln_mix, ln_ffn, hg_w_in, hg_lower_bound, hg_out_norm, hg_w_o, kv_norm, w_kv,
                    da_w_q, da_lambda_q1, da_lambda_k1, da_lambda_q2, da_lambda_k2, da_subln,
                    da_w_o, rel_bias, moe_w_group, moe_b_group, moe_w_expert, moe_b_expert,
                    moe_w_gate, moe_w_up, moe_w_down, final_norm,
                    attn_tile=min(512, x.shape[1]), moe_block=256)
```

```python
import functools
import math

import numpy as np
import jax
import jax.numpy as jnp
from jax import lax
from jax.experimental import pallas as pl
from jax.experimental.pallas import tpu as pltpu

F32 = jnp.float32
BF16 = jnp.bfloat16

NORM_EPS = 1e-6
HEAD_DIM = 128
HG_CHUNK = 64
HG_TILE = 256
HG_HEADS_PER_STEP = 4
MOE_GROUPS = 8
MOE_PER_GROUP = 8
MOE_EXPERTS = MOE_GROUPS * MOE_PER_GROUP
MOE_TOPK = 2
ATTN_ROW_GROUPS = 4
EXPERT_ROW_PARTS = 2
REL_BUCKETS = 32
REL_MAX_DIST = 128
NEG_BIG = -1e30
LOG2E = 1.4426950408889634
VMEM_LIMIT_BYTES = 56 * 1024 * 1024


def _params(*sem):
    return pltpu.CompilerParams(dimension_semantics=sem, vmem_limit_bytes=VMEM_LIMIT_BYTES)


def _sigmoid(x):
    return 1.0 / (1.0 + jnp.exp(-x))


def _dot(a, b):
    return jnp.dot(a, b, preferred_element_type=F32)


def _dot_nt(a, b):
    return lax.dot_general(a, b, (((1,), (1,)), ((), ())), preferred_element_type=F32)


def _dot_tn(a, b):
    return lax.dot_general(a, b, (((0,), (0,)), ((), ())), preferred_element_type=F32)


_HIGH_HALF = 0xFFFF0000


def _pack_halves(x):
    d = x.shape[1] // 2
    bits = lambda v: lax.bitcast_convert_type(v.astype(BF16).astype(F32), jnp.uint32)
    return (bits(x[:, :d]) >> 16) | (bits(x[:, d:]) & jnp.uint32(_HIGH_HALF))


def _unpack_halves(w):
    lo = lax.bitcast_convert_type(w << 16, F32)
    hi = lax.bitcast_convert_type(w & jnp.uint32(_HIGH_HALF), F32)
    return jnp.concatenate([lo, hi], axis=1)


def _norm_kernel(*refs, n_in, n_gain, emit_sum):
    in_refs = refs[:n_in]
    g_ref = refs[n_in]
    out_refs = refs[n_in + 1:]
    x = in_refs[0][...].astype(F32)
    for r in in_refs[1:]:
        x = x + _unpack_halves(r[...])
    k = 0
    if emit_sum:
        out_refs[0][...] = x
        k = 1
    xn = x * lax.rsqrt(jnp.mean(x * x, axis=-1, keepdims=True) + NORM_EPS)
    for j in range(n_gain):
        out_refs[k + j][...] = (xn * g_ref[j:j + 1, :]).astype(out_refs[k + j].dtype)


def _norm(h, gains, out_dtypes, *, y=None, emit_sum=False, tr=256):
    S, D = h.shape
    n_gain = gains.shape[0]
    ins = [h]
    in_specs = [pl.BlockSpec((tr, D), lambda i: (i, 0))]
    if y is not None:
        ins += [y, y]
        in_specs += [pl.BlockSpec((tr, D // 2), lambda i: (i, 0)),
                     pl.BlockSpec((tr, D // 2), lambda i: (S // tr + i, 0))]
    ins.append(gains)
    in_specs.append(pl.BlockSpec((n_gain, D), lambda i: (0, 0)))
    out_shape = []
    if emit_sum:
        out_shape.append(jax.ShapeDtypeStruct((S, D), F32))
    out_shape += [jax.ShapeDtypeStruct((S, D), dt) for dt in out_dtypes]
    out_specs = [pl.BlockSpec((tr, D), lambda i: (i, 0)) for _ in out_shape]
    return pl.pallas_call(
        functools.partial(_norm_kernel, n_in=len(ins) - 1, n_gain=n_gain, emit_sum=emit_sum),
        grid=(S // tr,), in_specs=in_specs, out_specs=out_specs, out_shape=out_shape,
        compiler_params=_params("parallel"), name="norm",
    )(*ins)


def _mm_kernel(*refs, has_res, scale):
    a_ref, w_ref = refs[0], refs[1]
    o_ref = refs[-1]
    acc = _dot(a_ref[...], w_ref[...])
    if scale is not None:
        acc = acc * scale
    if has_res:
        acc = acc + refs[2][...]
    o_ref[...] = acc.astype(o_ref.dtype)


def _matmul(a, w, *, out_dtype, res=None, scale=None, tm=1024, tn=512, name="matmul"):
    M, K = a.shape
    N = w.shape[1]
    tm, tn = min(tm, M), min(tn, N)
    ins = [a, w]
    in_specs = [pl.BlockSpec((tm, K), lambda i, j: (i, 0)),
                pl.BlockSpec((K, tn), lambda i, j: (0, j))]
    if res is not None:
        ins.append(res)
        in_specs.append(pl.BlockSpec((tm, tn), lambda i, j: (i, j)))
    return pl.pallas_call(
        functools.partial(_mm_kernel, has_res=res is not None, scale=scale),
        grid=(M // tm, N // tn), in_specs=in_specs,
        out_specs=pl.BlockSpec((tm, tn), lambda i, j: (i, j)),
        out_shape=jax.ShapeDtypeStruct((M, N), out_dtype),
        compiler_params=_params("parallel", "arbitrary"), name=name,
    )(*ins)


_HG_LEVELS = (32, 16, 8, 4, 2, 1)


def _hgrn_decay_matrix():
    C = HG_CHUNK
    t = np.arange(C)[:, None]
    j = np.arange(C)[None, :]
    blocks = [(j <= t), (j > t)]
    for w in _HG_LEVELS:
        r = (t // (2 * w)) * (2 * w) + w - 1
        upper = (t // w) % 2 == 1
        blocks.append(np.where(upper, (j > r) & (j <= t), (j > t) & (j <= r)))
    return np.concatenate(blocks, axis=0).astype(np.float32)


def _hgrn_kernel(q_ref, f_ref, v_ref, g_ref, lb_ref, gain_ref, m_ref, o_ref, st_ref, *, heads):
    C = HG_CHUNK
    dk = HEAD_DIM

    @pl.when(pl.program_id(1) == 0)
    def _():
        st_ref[...] = jnp.zeros_like(st_ref)

    lbr = lb_ref[...].astype(F32)
    lbe = jnp.exp(lbr - jnp.max(lbr, axis=0, keepdims=True))
    lb_all = lbe[0:1, :] / jnp.sum(lbe, axis=0, keepdims=True)
    gain = gain_ref[...].astype(F32)
    dec = m_ref[...]

    row = lax.broadcasted_iota(jnp.int32, (C, dk), 0)
    ti = lax.broadcasted_iota(jnp.int32, (C, C), 0)
    si = lax.broadcasted_iota(jnp.int32, (C, C), 1)
    upper = [((row // w) % 2) == 1 for w in _HG_LEVELS]
    same = [(ti // (2 * w)) == (si // (2 * w)) for w in _HG_LEVELS]

    units = [(slice(c * C, (c + 1) * C), hh, slice(hh * dk, (hh + 1) * dk))
             for c in range(q_ref.shape[0] // C) for hh in range(heads)]
    qf, kk, v, sums = [], [], [], []
    for sl, hh, cols in units:
        lb = lb_all[:, cols]
        q = q_ref[sl, cols].astype(F32)
        qf.append(q * _sigmoid(q))
        fg = lb + (1.0 - lb) * _sigmoid(f_ref[sl, cols].astype(F32))
        logf = jnp.log(fg)
        kk.append(1.0 - fg)
        v.append(v_ref[sl, cols].astype(BF16))
        hi = logf.astype(BF16)
        lo = (logf - hi.astype(F32)).astype(BF16)
        sums2 = _dot(dec, jnp.concatenate([hi, lo], axis=1))
        sums.append(sums2[:, :dk] + sums2[:, dk:])

    a = [jnp.where(ti == si, jnp.sum(qf[u] * kk[u], axis=-1, keepdims=True), 0.0)
         for u in range(len(units))]
    for l in range(len(_HG_LEVELS)):
        for u in range(len(units)):
            e = jnp.exp(sums[u][(2 + l) * C:(3 + l) * C])
            qw = jnp.where(upper[l], qf[u] * e, 0.0).astype(BF16)
            kw = jnp.where(upper[l], 0.0, kk[u] * e).astype(BF16)
            a[u] = a[u] + jnp.where(same[l], _dot_nt(qw, kw), 0.0)

    intra = [_dot(a[u].astype(BF16), v[u]) for u in range(len(units))]
    qb = [(qf[u] * jnp.exp(sums[u][0:C])).astype(BF16) for u in range(len(units))]
    kd = [(kk[u] * jnp.exp(sums[u][C:2 * C])).astype(BF16) for u in range(len(units))]
    st = [st_ref[hh] for hh in range(heads)]
    for u, (sl, hh, cols) in enumerate(units):
        o = intra[u] + _dot_nt(qb[u], st[hh].astype(BF16))
        st[hh] = st[hh] * jnp.exp(sums[u][C - 1:C, :]) + _dot_tn(v[u], kd[u])
        g = g_ref[sl, cols].astype(F32)
        on = o * lax.rsqrt(jnp.mean(o * o, axis=-1, keepdims=True) + NORM_EPS) * gain
        o_ref[sl, cols] = (on * (g * _sigmoid(g))).astype(o_ref.dtype)
    for hh in range(heads):
        st_ref[hh] = st[hh]


def _hgrn2(proj, lower_bound, out_gain):
    S = proj.shape[0]
    D = proj.shape[1] // 4
    H = D // HEAD_DIM
    T = min(HG_TILE, S)
    hpb = min(HG_HEADS_PER_STEP, H)
    G = H // hpb
    w = hpb * HEAD_DIM
    dec = jnp.asarray(_hgrn_decay_matrix(), BF16)
    blk = lambda off: pl.BlockSpec((T, w), lambda h, i, off=off: (i, off + h))
    return pl.pallas_call(
        functools.partial(_hgrn_kernel, heads=hpb),
        grid=(G, S // T),
        in_specs=[blk(0), blk(G), blk(2 * G), blk(3 * G),
                  pl.BlockSpec((lower_bound.shape[0], w), lambda h, i: (0, h)),
                  pl.BlockSpec((1, HEAD_DIM), lambda h, i: (0, 0)),
                  pl.BlockSpec(dec.shape, lambda h, i: (0, 0))],
        out_specs=pl.BlockSpec((T, w), lambda h, i: (i, h)),
        out_shape=jax.ShapeDtypeStruct((S, D), BF16),
        scratch_shapes=[pltpu.VMEM((hpb, HEAD_DIM, HEAD_DIM), F32)],
        compiler_params=_params("parallel", "arbitrary"), name="hgrn2",
    )(proj, proj, proj, proj, lower_bound, out_gain.reshape(1, HEAD_DIM), dec)


def _rep(x, n):
    return x if n == 1 else jnp.concatenate([x] * n, axis=1)


def _attn_kernel(q_ref, k_ref, v_ref, bias_ref, lam_ref, gain_ref, o_ref, m_ref, l_ref, acc_ref,
                 *, t, lambda_init):
    i = pl.program_id(1)
    dh = HEAD_DIM
    maps = range(2)
    m_ref[...] = jnp.full_like(m_ref, NEG_BIG)
    l_ref[...] = jnp.zeros_like(l_ref)
    acc_ref[...] = jnp.zeros_like(acc_ref)

    rg = t // ATTN_ROW_GROUPS
    chains = [(m, pl.ds(g * rg, rg)) for g in range(ATTN_ROW_GROUPS) for m in maps]

    def span(j, width, bias):
        r0 = pl.multiple_of(j * t, t)
        vt = v_ref[pl.ds(r0, width), :]
        s = [_dot_nt(q_ref[rows, m * dh:(m + 1) * dh], k_ref[pl.ds(r0, width), m * dh:(m + 1) * dh])
             for m, rows in chains]
        if bias is not None:
            s = [s[c] + bias(m, rows) for c, (m, rows) in enumerate(chains)]
        m_prev = [m_ref[m, rows, :] for m, rows in chains]
        m_next = [jnp.maximum(m_prev[c], jnp.max(s[c], axis=-1, keepdims=True)) for c in range(len(chains))]
        p = [jnp.exp2(s[c] - _rep(m_next[c], width // 128)) for c in range(len(chains))]
        alpha = [jnp.exp2(m_prev[c] - m_next[c]) for c in range(len(chains))]
        pv = [_dot(p[c].astype(BF16), vt) for c in range(len(chains))]
        for c, (m, rows) in enumerate(chains):
            l_ref[m, rows, :] = alpha[c] * l_ref[m, rows, :] + jnp.sum(p[c], axis=-1, keepdims=True)
            acc_ref[m, rows, :] = acc_ref[m, rows, :] * _rep(alpha[c], 2) + pv[c]
            m_ref[m, rows, :] = m_next[c]

    n_far = jnp.maximum(i - 1, 0)

    def far_pair(jj, carry):
        span(2 * jj, 2 * t, None)
        return carry

    lax.fori_loop(0, n_far // 2, far_pair, 0)

    @pl.when(n_far % 2 == 1)
    def _():
        span(i - 2, t, None)

    @pl.when(i == 0)
    def _():
        span(i, t, lambda m, rows: bias_ref[m, rows, t:2 * t])

    @pl.when(i >= 1)
    def _():
        span(i - 1, 2 * t, lambda m, rows: bias_ref[m, rows, :])

    lam_p = lam_ref[...].astype(F32)
    lam = (jnp.exp(jnp.sum(lam_p[0:1] * lam_p[1:2], axis=-1, keepdims=True))
           - jnp.exp(jnp.sum(lam_p[2:3] * lam_p[3:4], axis=-1, keepdims=True)) + lambda_init)
    o0 = acc_ref[0] * _rep(1.0 / l_ref[0], 2)
    o1 = acc_ref[1] * _rep(1.0 / l_ref[1], 2)
    att = o0 - lam * o1
    att = att * lax.rsqrt(jnp.mean(att * att, axis=-1, keepdims=True) + NORM_EPS)
    o_ref[...] = (att * gain_ref[...].astype(F32) * (1.0 - lambda_init)).astype(o_ref.dtype)


def _t5_bucket(dist):
    max_exact = REL_BUCKETS // 2
    d = jnp.maximum(dist, 1).astype(F32)
    large = max_exact + (jnp.log(d / max_exact) / math.log(REL_MAX_DIST / max_exact)
                         * (REL_BUCKETS - max_exact)).astype(jnp.int32)
    large = jnp.minimum(large, REL_BUCKETS - 1)
    return jnp.where(dist < max_exact, dist, large)


def _attn_bias_tiles(rel_bias, t):
    n = REL_MAX_DIST
    p = 2 * n
    nb = t // n
    table = rel_bias.astype(F32)[_t5_bucket(jnp.arange(p))]
    table = ((table - table[n][None]) * LOG2E).transpose(1, 2, 0)
    rot = jnp.tile(table, (1, 1, n + 1))[:, :, :n * (p + 1)].reshape(*table.shape[:2], n, p + 1)
    band0 = jnp.flip(rot[..., p - n + 1:p + 1], axis=-1)
    band1 = jnp.flip(rot[..., 1:n + 1], axis=-1)
    r = jnp.arange(t)[:, None]
    c = jnp.arange(t)[None, :]
    rb, cb = r // n, c // n
    band0, band1 = jnp.tile(band0, (1, 1, nb, nb)), jnp.tile(band1, (1, 1, nb, nb))
    diag = jnp.where(r < c, NEG_BIG,
                     jnp.where(rb == cb, band0, jnp.where(rb == cb + 1, band1, 0.0)))
    sub = jnp.where((rb == 0) & (cb == nb - 1), band1, 0.0)
    return jnp.concatenate([sub, diag], axis=-1)


def _diff_attention(q, kv, bias, lam_params, gain, lambda_init, *, t):
    S, D = q.shape
    hw = 2 * HEAD_DIM
    H = D // hw
    per_head = dict(pipeline_mode=pl.Buffered(1))
    return pl.pallas_call(
        functools.partial(_attn_kernel, t=t, lambda_init=lambda_init),
        grid=(H, S // t),
        in_specs=[pl.BlockSpec((t, hw), lambda h, i: (i, h)),
                  pl.BlockSpec((S, hw), lambda h, i: (0, h), **per_head),
                  pl.BlockSpec((S, hw), lambda h, i: (0, H + h), **per_head),
                  pl.BlockSpec((None, 2, t, 2 * t), lambda h, i: (h, 0, 0, 0), **per_head),
                  pl.BlockSpec((4, HEAD_DIM), lambda h, i: (0, 0)),
                  pl.BlockSpec((1, hw), lambda h, i: (0, 0))],
        out_specs=pl.BlockSpec((t, hw), lambda h, i: (i, h)),
        out_shape=jax.ShapeDtypeStruct((S, D), BF16),
        scratch_shapes=[pltpu.VMEM((2, t, 128), F32), pltpu.VMEM((2, t, 128), F32),
                        pltpu.VMEM((2, t, hw), F32)],
        compiler_params=_params("parallel", "arbitrary"), name="diff_attn",
    )(q, kv, kv, bias, lam_params, gain.reshape(1, hw))


def _router_kernel(h_ref, g_ref, w_ref, b_ref, xn_ref, ids_ref, wts_ref):
    x = h_ref[...]
    xn = x * lax.rsqrt(jnp.mean(x * x, axis=-1, keepdims=True) + NORM_EPS) * g_ref[...]
    xn_ref[...] = _pack_halves(xn)
    xh = xn.astype(BF16)
    xl = (xn - xh.astype(F32)).astype(BF16)
    w = w_ref[...]
    wh = w.astype(BF16)
    wl = (w - wh.astype(F32)).astype(BF16)
    logits = _dot(xh, wh) + _dot(xh, wl) + _dot(xl, wh) + b_ref[...]

    lane = lax.broadcasted_iota(jnp.int32, logits.shape, 1).astype(F32)
    first = lambda hit: jnp.min(jnp.where(hit, lane, 1e9), axis=-1, keepdims=True)
    gl = jnp.where(lane < MOE_GROUPS, logits, NEG_BIG)
    gmax = jnp.max(gl, axis=-1, keepdims=True)
    g_w = 1.0 / jnp.sum(jnp.exp(gl - gmax), axis=-1, keepdims=True)
    g_idx = first(gl == gmax)
    lo = MOE_GROUPS + MOE_PER_GROUP * g_idx
    el = jnp.where((lane >= lo) & (lane < lo + MOE_PER_GROUP), logits, NEG_BIG)
    e1 = jnp.max(el, axis=-1, keepdims=True)
    i1 = first(el == e1)
    el2 = jnp.where(lane == i1, NEG_BIG, el)
    e2 = jnp.max(el2, axis=-1, keepdims=True)
    i2 = first(el2 == e2)
    r = jnp.exp(e2 - e1)
    w1 = g_w / (1.0 + r)
    w2 = g_w * r / (1.0 + r)
    ids = jnp.where(lane == 0, i1, jnp.where(lane == 1, i2, float(MOE_GROUPS))) - MOE_GROUPS
    ids_ref[...] = ids.astype(jnp.int32)
    wts_ref[...] = jnp.where(lane == 0, w1, jnp.where(lane == 1, w2, 0.0))


def _router(h, gain, w_group, b_group, w_expert, b_expert, *, tr=256):
    S, D = h.shape
    n_used = MOE_GROUPS + MOE_EXPERTS
    w = jnp.pad(jnp.concatenate([w_group, w_expert], axis=1).astype(F32), ((0, 0), (0, 128 - n_used)))
    b = jnp.pad(jnp.concatenate([b_group.reshape(-1), b_expert.reshape(-1)]).astype(F32),
                (0, 128 - n_used)).reshape(1, 128)
    row = lambda width: pl.BlockSpec((tr, width), lambda i: (i, 0))
    full = lambda shape: pl.BlockSpec(shape, lambda i: (0, 0))
    xn, ids, wts = pl.pallas_call(
        _router_kernel, grid=(S // tr,),
        in_specs=[row(D), full((1, D)), full((D, 128)), full((1, 128))],
        out_specs=[row(D // 2), row(128), row(128)],
        out_shape=[jax.ShapeDtypeStruct((S, D // 2), jnp.uint32), jax.ShapeDtypeStruct((S, 128), jnp.int32),
                   jax.ShapeDtypeStruct((S, 128), F32)],
        compiler_params=_params("parallel"), name="moe_router",
    )(h, gain.reshape(1, D), w, b)
    return xn, ids[:, :MOE_TOPK], wts[:, :MOE_TOPK]


def _dispatch(expert, weight, bm, n_tok):
    n_asg = n_tok * MOE_TOPK
    E = MOE_EXPERTS
    flat_e = expert.reshape(n_asg)
    onehot = (flat_e[:, None] == jnp.arange(E, dtype=jnp.int32)[None, :]).astype(jnp.int32)
    counts = jnp.sum(onehot, axis=0)
    rank = jnp.sum(jnp.cumsum(onehot, axis=0) * onehot, axis=1) - 1
    padded = (counts + bm - 1) // bm * bm
    pad_end = jnp.cumsum(padded)
    pad_start = pad_end - padded
    dest = jnp.sum(onehot * pad_start[None, :], axis=1) + rank
    n_blocks = -(-n_asg // bm) + E
    n_slots = n_blocks * bm
    n = jnp.arange(n_asg, dtype=jnp.int32)
    tok = n // MOE_TOPK
    row = (n % MOE_TOPK) * n_tok + tok
    w_bits = lax.bitcast_convert_type(weight.reshape(n_asg).astype(F32), jnp.int32)
    s = jnp.arange(n_slots, dtype=jnp.int32)
    spare = MOE_TOPK * n_tok + ((s // bm) % 2) * bm + s % bm
    empty = jnp.stack([jnp.zeros_like(s), spare, jnp.zeros_like(s)], axis=1)
    slots = empty.at[dest].set(jnp.stack([tok, row, w_bits], axis=1))
    block_start = jnp.arange(n_blocks, dtype=jnp.int32) * bm
    before = (pad_end[None, :] <= block_start[:, None]).astype(jnp.int32)
    block_e = jnp.minimum(jnp.sum(before, axis=1), E - 1)
    live = (block_start < pad_end[-1]).astype(jnp.int32)
    slot_w = lax.bitcast_convert_type(slots[:, 2], F32)
    return (block_e.astype(jnp.int32), live, slots[:, 0].reshape(n_blocks, 1, bm),
            slots[:, 1].reshape(n_blocks, 1, bm), slot_w.reshape(n_slots, 1), n_blocks)


def _expert_kernel(be_ref, live_ref, src_ref, nsrc_ref, pdst_ref, w_ref, x_hbm, wg_ref, wu_ref, wd_ref,
                   y_hbm, xbuf, ybuf, gsem, ssem, *, bm, n_rows):
    b = pl.program_id(0)
    live = live_ref[b] == 1
    prev_live = (b >= 1) & (live_ref[jnp.maximum(b - 1, 0)] == 1)
    ff = wd_ref.shape[0]
    lane = 128

    def gather_rows(idx_ref, buf_slot):
        def start(r):
            pltpu.make_async_copy(x_hbm.at[pl.ds(idx_ref[0, 0, r], 1), :],
                                  xbuf.at[buf_slot, pl.ds(r, 1), :], gsem.at[buf_slot]).start()
        return [functools.partial(start, r) for r in range(bm)]

    def scatter_rows(idx_ref, buf_slot):
        def start(r):
            pltpu.make_async_copy(ybuf.at[buf_slot, pl.ds(r, 1), :],
                                  y_hbm.at[pl.ds(idx_ref[0, 0, r], 1), :], ssem.at[buf_slot]).start()
        return [functools.partial(start, r) for r in range(bm)]

    def wait_block(sem, buf):
        pltpu.make_async_copy(buf, buf, sem).wait()

    def ffn(slot, starts):
        if ff % (2 * lane) == 0:
            w_cols = [lambda: wg_ref[...].astype(BF16), lambda: wu_ref[...].astype(BF16)]
        else:
            w_cols = [lambda: wg_ref[:, :ff - lane].astype(BF16),
                      lambda: jnp.concatenate([wg_ref[:, ff - lane:], wu_ref[:, :lane]], axis=1).astype(BF16),
                      lambda: wu_ref[:, lane:].astype(BF16)]
        n_dots = EXPERT_ROW_PARTS * (len(w_cols) + 1)
        share = -(-len(starts) // n_dots)
        starts = list(starts)

        def dot_with_copies(a, w):
            for start in starts[:share]:
                start()
            del starts[:share]
            return _dot(a, w())

        part = bm // EXPERT_ROW_PARTS
        for i in range(EXPERT_ROW_PARTS):
            rows = pl.ds(i * part, part)
            xb = _unpack_halves(xbuf[slot, rows, :]).astype(BF16)
            gu = jnp.concatenate([dot_with_copies(xb, w) for w in w_cols], axis=1)
            gate, up = gu[:, :ff], gu[:, ff:]
            hid = (gate * _sigmoid(gate)) * up
            y = dot_with_copies(hid.astype(BF16), lambda: wd_ref[...].astype(BF16))
            ybuf[slot, rows, :] = _pack_halves(y * w_ref[rows, :])
        assert not starts

    @pl.when(live & (b == 0))
    def _():
        ybuf[...] = jnp.zeros_like(ybuf)
        for p in range(2):
            spare = pltpu.make_async_copy(ybuf.at[p], y_hbm.at[pl.ds(n_rows + p * bm, bm), :], ssem.at[p])
            spare.start()
            spare.wait()
        for start in gather_rows(src_ref, 0):
            start()
        wait_block(gsem.at[0], xbuf.at[0])
        ffn(0, gather_rows(nsrc_ref, 1))

    for slot in range(2):
        @pl.when(live & (b >= 1) & (b % 2 == slot))
        def _(slot=slot):
            wait_block(gsem.at[slot], xbuf.at[slot])
            pl.when(b >= 2)(lambda: wait_block(ssem.at[slot], ybuf.at[slot]))
            ffn(slot, gather_rows(nsrc_ref, 1 - slot) + scatter_rows(pdst_ref, 1 - slot))

        @pl.when(jnp.logical_not(live) & prev_live & (b % 2 == slot))
        def _(slot=slot):
            wait_block(gsem.at[slot], xbuf.at[slot])
            pl.when(b >= 2)(lambda: wait_block(ssem.at[slot], ybuf.at[slot]))
            for start in scatter_rows(pdst_ref, 1 - slot):
                start()
            wait_block(ssem.at[1 - slot], ybuf.at[1 - slot])


def _moe_experts(xn, expert, weight, w_gate, w_up, w_down, layer, *, bm):
    S, half = xn.shape
    D = 2 * half
    FF = w_down.shape[2]
    assert FF % 128 == 0
    be, live, src, dst, sw, n_blocks = _dispatch(expert, weight, bm, S)
    live = jnp.concatenate([live, jnp.zeros((1,), jnp.int32)])
    last = n_blocks - 1
    at = lambda off: (lambda b, be, live: (jnp.clip(b + off, 0, last), 0, 0))
    smem = lambda off: pl.BlockSpec((1, 1, bm), at(off), memory_space=pltpu.SMEM)
    wspec = lambda shape: pl.BlockSpec((None, None) + shape,
                                       lambda b, be, live: (layer, be[jnp.minimum(b, last)], 0, 0))
    return pl.pallas_call(
        functools.partial(_expert_kernel, bm=bm, n_rows=MOE_TOPK * S),
        grid_spec=pltpu.PrefetchScalarGridSpec(
            num_scalar_prefetch=2, grid=(n_blocks + 1,),
            in_specs=[smem(0), smem(1), smem(-1),
                      pl.BlockSpec((bm, 1), lambda b, be, live: (jnp.minimum(b, last), 0)),
                      pl.BlockSpec(memory_space=pl.ANY),
                      wspec((D, FF)), wspec((D, FF)), wspec((FF, D))],
            out_specs=pl.BlockSpec(memory_space=pl.ANY),
            scratch_shapes=[pltpu.VMEM((2, bm, half), jnp.uint32), pltpu.VMEM((2, bm, half), jnp.uint32),
                            pltpu.SemaphoreType.DMA((2,)), pltpu.SemaphoreType.DMA((2,))]),
        out_shape=jax.ShapeDtypeStruct((MOE_TOPK * S + 2 * bm, half), jnp.uint32),
        compiler_params=_params("arbitrary"), name="moe_experts",
    )(be, live, src, src, dst, sw, xn, w_gate, w_up, w_down)


def _moe(h, gain, w_group, b_group, w_expert, b_expert, w_gate, w_up, w_down, layer, *, bm):
    xn, expert, weight = _router(h, gain, w_group, b_group, w_expert, b_expert)
    return _moe_experts(xn, expert, weight, w_gate, w_up, w_down, layer, bm=bm)


def _forward(x, ln_mix, ln_ffn, hg_w_in, hg_lower_bound, hg_out_norm, hg_w_o, kv_norm, w_kv,
             da_w_q, da_lambda_q1, da_lambda_k1, da_lambda_q2, da_lambda_k2, da_subln, da_w_o,
             rel_bias, moe_w_group, moe_b_group, moe_w_expert, moe_b_expert, moe_w_gate,
             moe_w_up, moe_w_down, final_norm, *, attn_tile, moe_block):
    B, S, D = x.shape
    assert B == 1 and ln_mix.shape[0] == 2 and hg_w_in.shape[0] == 1 and da_w_q.shape[0] == 1
    h = x.reshape(S, D)

    (xn,) = _norm(h, ln_mix[0:1], [BF16])
    proj = _matmul(xn, hg_w_in[0].astype(BF16), out_dtype=BF16, name="hg_in_proj")
    mix = _hgrn2(proj, hg_lower_bound, hg_out_norm[0])
    h = _matmul(mix, hg_w_o[0].astype(BF16), out_dtype=F32, res=h, name="hg_out_proj")
    wg, wu, wd = moe_w_gate, moe_w_up, moe_w_down
    y = _moe(h, ln_ffn[0], moe_w_group[0], moe_b_group[0], moe_w_expert[0], moe_b_expert[0],
             wg, wu, wd, 0, bm=moe_block)

    layer = 1
    lambda_init = 0.8 - 0.6 * math.exp(-0.3 * layer)
    h, xkv, xq = _norm(h, jnp.stack([kv_norm, ln_mix[1]]), [BF16, BF16], y=y, emit_sum=True)
    kv = _matmul(xkv, w_kv.astype(BF16), out_dtype=BF16, name="kv_proj")
    q = _matmul(xq, da_w_q[0].astype(BF16), out_dtype=BF16, scale=HEAD_DIM ** -0.5 * LOG2E,
                name="q_proj")
    lam_params = jnp.stack([da_lambda_q1[0], da_lambda_k1[0], da_lambda_q2[0], da_lambda_k2[0]])
    att = _diff_attention(q, kv, _attn_bias_tiles(rel_bias, attn_tile), lam_params, da_subln[0],
                          lambda_init, t=attn_tile)
    h = _matmul(att, da_w_o[0].astype(BF16), out_dtype=F32, res=h, name="da_out_proj")
    y = _moe(h, ln_ffn[1], moe_w_group[1], moe_b_group[1], moe_w_expert[1], moe_b_expert[1],
             wg, wu, wd, 1, bm=moe_block)

    (out,) = _norm(h, final_norm.reshape(1, D), [F32], y=y)
    return out.reshape(B, S, D)


def kernel(x, ln_mix, ln_ffn, hg_w_in, hg_lower_bound, hg_out_norm, hg_w_o, kv_norm, w_kv, da_w_q, da_lambda_q1, da_lambda_k1, da_lambda_q2, da_lambda_k2, da_subln, da_w_o, rel_bias, moe_w_group, moe_b_group, moe_w_expert, moe_b_expert, moe_w_gate, moe_w_up, moe_w_down, final_norm):
    return _forward(x, ln_mix, ln_ffn, hg_w_in, hg_lower_bound, hg_out_norm, hg_w_o, kv_norm, w_kv,
                    da_w_q, da_lambda_q1, da_lambda_k1, da_lambda_q2, da_lambda_k2, da_subln,
                    da_w_o, rel_bias, moe_w_group, moe_b_group, moe_w_expert, moe_b_expert,
                    moe_w_gate, moe_w_up, moe_w_down, final_norm,
                    attn_tile=min(512, x.shape[1]), moe_block=256)
```

```python
import functools
import math

import numpy as np
import jax
import jax.numpy as jnp
from jax import lax
from jax.experimental import pallas as pl
from jax.experimental.pallas import tpu as pltpu

F32 = jnp.float32
BF16 = jnp.bfloat16

NORM_EPS = 1e-6
HEAD_DIM = 128
HG_CHUNK = 64
HG_TILE = 256
HG_HEADS_PER_STEP = 4
MOE_GROUPS = 8
MOE_PER_GROUP = 8
MOE_EXPERTS = MOE_GROUPS * MOE_PER_GROUP
MOE_TOPK = 2
ATTN_ROW_GROUPS = 1
EXPERT_ROW_PARTS = 2
REL_BUCKETS = 32
REL_MAX_DIST = 128
NEG_BIG = -1e30
LOG2E = 1.4426950408889634
VMEM_LIMIT_BYTES = 56 * 1024 * 1024


def _params(*sem):
    return pltpu.CompilerParams(dimension_semantics=sem, vmem_limit_bytes=VMEM_LIMIT_BYTES)


def _sigmoid(x):
    return 1.0 / (1.0 + jnp.exp(-x))


def _dot(a, b):
    return jnp.dot(a, b, preferred_element_type=F32)


def _dot_nt(a, b):
    return lax.dot_general(a, b, (((1,), (1,)), ((), ())), preferred_element_type=F32)


def _dot_tn(a, b):
    return lax.dot_general(a, b, (((0,), (0,)), ((), ())), preferred_element_type=F32)


_HIGH_HALF = 0xFFFF0000


def _pack_halves(x):
    d = x.shape[1] // 2
    bits = lambda v: lax.bitcast_convert_type(v.astype(BF16).astype(F32), jnp.uint32)
    return (bits(x[:, :d]) >> 16) | (bits(x[:, d:]) & jnp.uint32(_HIGH_HALF))


def _unpack_halves(w):
    lo = lax.bitcast_convert_type(w << 16, F32)
    hi = lax.bitcast_convert_type(w & jnp.uint32(_HIGH_HALF), F32)
    return jnp.concatenate([lo, hi], axis=1)


def _norm_kernel(*refs, n_in, n_gain, emit_sum):
    in_refs = refs[:n_in]
    g_ref = refs[n_in]
    out_refs = refs[n_in + 1:]
    x = in_refs[0][...].astype(F32)
    for r in in_refs[1:]:
        x = x + _unpack_halves(r[...])
    k = 0
    if emit_sum:
        out_refs[0][...] = x
        k = 1
    xn = x * lax.rsqrt(jnp.mean(x * x, axis=-1, keepdims=True) + NORM_EPS)
    for j in range(n_gain):
        out_refs[k + j][...] = (xn * g_ref[j:j + 1, :]).astype(out_refs[k + j].dtype)


def _norm(h, gains, out_dtypes, *, y=None, emit_sum=False, tr=256):
    S, D = h.shape
    n_gain = gains.shape[0]
    ins = [h]
    in_specs = [pl.BlockSpec((tr, D), lambda i: (i, 0))]
    if y is not None:
        ins += [y, y]
        in_specs += [pl.BlockSpec((tr, D // 2), lambda i: (i, 0)),
                     pl.BlockSpec((tr, D // 2), lambda i: (S // tr + i, 0))]
    ins.append(gains)
    in_specs.append(pl.BlockSpec((n_gain, D), lambda i: (0, 0)))
    out_shape = []
    if emit_sum:
        out_shape.append(jax.ShapeDtypeStruct((S, D), F32))
    out_shape += [jax.ShapeDtypeStruct((S, D), dt) for dt in out_dtypes]
    out_specs = [pl.BlockSpec((tr, D), lambda i: (i, 0)) for _ in out_shape]
    return pl.pallas_call(
        functools.partial(_norm_kernel, n_in=len(ins) - 1, n_gain=n_gain, emit_sum=emit_sum),
        grid=(S // tr,), in_specs=in_specs, out_specs=out_specs, out_shape=out_shape,
        compiler_params=_params("parallel"), name="norm",
    )(*ins)


def _mm_kernel(*refs, has_res, scale):
    a_ref, w_ref = refs[0], refs[1]
    o_ref = refs[-1]
    acc = _dot(a_ref[...], w_ref[...])
    if scale is not None:
        acc = acc * scale
    if has_res:
        acc = acc + refs[2][...]
    o_ref[...] = acc.astype(o_ref.dtype)


def _matmul(a, w, *, out_dtype, res=None, scale=None, tm=1024, tn=512, name="matmul"):
    M, K = a.shape
    N = w.shape[1]
    tm, tn = min(tm, M), min(tn, N)
    ins = [a, w]
    in_specs = [pl.BlockSpec((tm, K), lambda i, j: (i, 0)),
                pl.BlockSpec((K, tn), lambda i, j: (0, j))]
    if res is not None:
        ins.append(res)
        in_specs.append(pl.BlockSpec((tm, tn), lambda i, j: (i, j)))
    return pl.pallas_call(
        functools.partial(_mm_kernel, has_res=res is not None, scale=scale),
        grid=(M // tm, N // tn), in_specs=in_specs,
        out_specs=pl.BlockSpec((tm, tn), lambda i, j: (i, j)),
        out_shape=jax.ShapeDtypeStruct((M, N), out_dtype),
        compiler_params=_params("parallel", "arbitrary"), name=name,
    )(*ins)


_HG_LEVELS = (32, 16, 8, 4, 2, 1)


def _hgrn_decay_matrix():
    C = HG_CHUNK
    t = np.arange(C)[:, None]
    j = np.arange(C)[None, :]
    blocks = [(j <= t), (j > t)]
    for w in _HG_LEVELS:
        r = (t // (2 * w)) * (2 * w) + w - 1
        upper = (t // w) % 2 == 1
        blocks.append(np.where(upper, (j > r) & (j <= t), (j > t) & (j <= r)))
    return np.concatenate(blocks, axis=0).astype(np.float32)


def _hgrn_kernel(q_ref, f_ref, v_ref, g_ref, lb_ref, gain_ref, m_ref, o_ref, st_ref, *, heads):
    C = HG_CHUNK
    dk = HEAD_DIM

    @pl.when(pl.program_id(1) == 0)
    def _():
        st_ref[...] = jnp.zeros_like(st_ref)

    lbr = lb_ref[...].astype(F32)
    lbe = jnp.exp(lbr - jnp.max(lbr, axis=0, keepdims=True))
    lb_all = lbe[0:1, :] / jnp.sum(lbe, axis=0, keepdims=True)
    gain = gain_ref[...].astype(F32)
    dec = m_ref[...]

    row = lax.broadcasted_iota(jnp.int32, (C, dk), 0)
    ti = lax.broadcasted_iota(jnp.int32, (C, C), 0)
    si = lax.broadcasted_iota(jnp.int32, (C, C), 1)
    upper = [((row // w) % 2) == 1 for w in _HG_LEVELS]
    same = [(ti // (2 * w)) == (si // (2 * w)) for w in _HG_LEVELS]

    units = [(slice(c * C, (c + 1) * C), hh, slice(hh * dk, (hh + 1) * dk))
             for c in range(q_ref.shape[0] // C) for hh in range(heads)]
    qf, kk, v, sums = [], [], [], []
    for sl, hh, cols in units:
        lb = lb_all[:, cols]
        q = q_ref[sl, cols].astype(F32)
        qf.append(q * _sigmoid(q))
        fg = lb + (1.0 - lb) * _sigmoid(f_ref[sl, cols].astype(F32))
        logf = jnp.log(fg)
        kk.append(1.0 - fg)
        v.append(v_ref[sl, cols].astype(BF16))
        hi = logf.astype(BF16)
        lo = (logf - hi.astype(F32)).astype(BF16)
        sums2 = _dot(dec, jnp.concatenate([hi, lo], axis=1))
        sums.append(sums2[:, :dk] + sums2[:, dk:])

    a = [jnp.where(ti == si, jnp.sum(qf[u] * kk[u], axis=-1, keepdims=True), 0.0)
         for u in range(len(units))]
    for l in range(len(_HG_LEVELS)):
        for u in range(len(units)):
            e = jnp.exp(sums[u][(2 + l) * C:(3 + l) * C])
            qw = jnp.where(upper[l], qf[u] * e, 0.0).astype(BF16)
            kw = jnp.where(upper[l], 0.0, kk[u] * e).astype(BF16)
            a[u] = a[u] + jnp.where(same[l], _dot_nt(qw, kw), 0.0)

    intra = [_dot(a[u].astype(BF16), v[u]) for u in range(len(units))]
    qb = [(qf[u] * jnp.exp(sums[u][0:C])).astype(BF16) for u in range(len(units))]
    kd = [(kk[u] * jnp.exp(sums[u][C:2 * C])).astype(BF16) for u in range(len(units))]
    st = [st_ref[hh] for hh in range(heads)]
    for u, (sl, hh, cols) in enumerate(units):
        o = intra[u] + _dot_nt(qb[u], st[hh].astype(BF16))
        st[hh] = st[hh] * jnp.exp(sums[u][C - 1:C, :]) + _dot_tn(v[u], kd[u])
        g = g_ref[sl, cols].astype(F32)
        on = o * lax.rsqrt(jnp.mean(o * o, axis=-1, keepdims=True) + NORM_EPS) * gain
        o_ref[sl, cols] = (on * (g * _sigmoid(g))).astype(o_ref.dtype)
    for hh in range(heads):
        st_ref[hh] = st[hh]


def _hgrn2(proj, lower_bound, out_gain):
    S = proj.shape[0]
    D = proj.shape[1] // 4
    H = D // HEAD_DIM
    T = min(HG_TILE, S)
    hpb = min(HG_HEADS_PER_STEP, H)
    G = H // hpb
    w = hpb * HEAD_DIM
    dec = jnp.asarray(_hgrn_decay_matrix(), BF16)
    blk = lambda off: pl.BlockSpec((T, w), lambda h, i, off=off: (i, off + h))
    return pl.pallas_call(
        functools.partial(_hgrn_kernel, heads=hpb),
        grid=(G, S // T),
        in_specs=[blk(0), blk(G), blk(2 * G), blk(3 * G),
                  pl.BlockSpec((lower_bound.shape[0], w), lambda h, i: (0, h)),
                  pl.BlockSpec((1, HEAD_DIM), lambda h, i: (0, 0)),
                  pl.BlockSpec(dec.shape, lambda h, i: (0, 0))],
        out_specs=pl.BlockSpec((T, w), lambda h, i: (i, h)),
        out_shape=jax.ShapeDtypeStruct((S, D), BF16),
        scratch_shapes=[pltpu.VMEM((hpb, HEAD_DIM, HEAD_DIM), F32)],
        compiler_params=_params("parallel", "arbitrary"), name="hgrn2",
    )(proj, proj, proj, proj, lower_bound, out_gain.reshape(1, HEAD_DIM), dec)


def _rep(x, n):
    return x if n == 1 else jnp.concatenate([x] * n, axis=1)


def _attn_kernel(q_ref, k_ref, v_ref, bias_ref, lam_ref, gain_ref, o_ref, m_ref, l_ref, acc_ref,
                 s_ref, r_ref, *, t, lambda_init):
    i = pl.program_id(1)
    dh = HEAD_DIM
    maps = range(2)
    m_ref[...] = jnp.full_like(m_ref, NEG_BIG)
    l_ref[...] = jnp.zeros_like(l_ref)
    acc_ref[...] = jnp.zeros_like(acc_ref)

    rg = t // ATTN_ROW_GROUPS
    chains = [(m, pl.ds(g * rg, rg)) for g in range(ATTN_ROW_GROUPS) for m in maps]

    def scores(j, width, bias):
        r0 = pl.multiple_of(j * t, t)
        s = [_dot_nt(q_ref[rows, m * dh:(m + 1) * dh], k_ref[pl.ds(r0, width), m * dh:(m + 1) * dh])
             for m, rows in chains]
        if bias is not None:
            s = [s[c] + bias(m, rows) for c, (m, rows) in enumerate(chains)]
        s = [x.astype(BF16) for x in s]
        return s, [jnp.max(x, axis=-1, keepdims=True).astype(F32) for x in s]

    def absorb(j, width, s, row_max):
        vt = v_ref[pl.ds(pl.multiple_of(j * t, t), width), :]
        m_prev = [m_ref[m, rows, :] for m, rows in chains]
        m_next = [jnp.maximum(m_prev[c], row_max[c]) for c in range(len(chains))]
        p = [jnp.exp2(s[c] - _rep(m_next[c].astype(BF16), width // 128)) for c in range(len(chains))]
        alpha = [jnp.exp2(m_prev[c] - m_next[c]) for c in range(len(chains))]
        pv = [_dot(p[c], vt) for c in range(len(chains))]
        for c, (m, rows) in enumerate(chains):
            blocks = [p[c][:, k * 128:(k + 1) * 128] for k in range(width // 128)]
            for _ in range(2):
                if len(blocks) % 2 == 0:
                    blocks = [blocks[k] + blocks[k + 1] for k in range(0, len(blocks), 2)]
            row_sum = jnp.sum(sum(b.astype(F32) for b in blocks), axis=-1, keepdims=True)
            l_ref[m, rows, :] = alpha[c] * l_ref[m, rows, :] + row_sum
            acc_ref[m, rows, :] = acc_ref[m, rows, :] * _rep(alpha[c], 2) + pv[c]
            m_ref[m, rows, :] = m_next[c]

    def span(j, width, bias):
        absorb(j, width, *scores(j, width, bias))

    n_far = jnp.maximum(i - 1, 0)
    n_span = n_far // 2

    def produce(k, slot):
        s, row_max = scores(2 * jnp.minimum(k, n_span - 1), 2 * t, None)
        for c in range(len(chains)):
            s_ref[slot, c] = s[c]
            r_ref[slot, c] = jnp.broadcast_to(row_max[c], (rg, 128))

    def consume(k, slot):
        absorb(2 * k, 2 * t, [s_ref[slot, c] for c in range(len(chains))],
               [r_ref[slot, c] for c in range(len(chains))])

    @pl.when(n_span >= 1)
    def _():
        produce(0, 0)

    def two_spans(it, carry):
        produce(2 * it + 1, 1)
        consume(2 * it, 0)
        produce(2 * it + 2, 0)
        consume(2 * it + 1, 1)
        return carry

    lax.fori_loop(0, n_span // 2, two_spans, 0)

    @pl.when(n_span % 2 == 1)
    def _():
        consume(n_span - 1, 0)

    @pl.when(n_far % 2 == 1)
    def _():
        span(i - 2, t, None)

    @pl.when(i == 0)
    def _():
        span(i, t, lambda m, rows: bias_ref[m, rows, t:2 * t])

    @pl.when(i >= 1)
    def _():
        span(i - 1, 2 * t, lambda m, rows: bias_ref[m, rows, :])

    lam_p = lam_ref[...].astype(F32)
    lam = (jnp.exp(jnp.sum(lam_p[0:1] * lam_p[1:2], axis=-1, keepdims=True))
           - jnp.exp(jnp.sum(lam_p[2:3] * lam_p[3:4], axis=-1, keepdims=True)) + lambda_init)
    o0 = acc_ref[0] * _rep(1.0 / l_ref[0], 2)
    o1 = acc_ref[1] * _rep(1.0 / l_ref[1], 2)
    att = o0 - lam * o1
    att = att * lax.rsqrt(jnp.mean(att * att, axis=-1, keepdims=True) + NORM_EPS)
    o_ref[...] = (att * gain_ref[...].astype(F32) * (1.0 - lambda_init)).astype(o_ref.dtype)


def _t5_bucket(dist):
    max_exact = REL_BUCKETS // 2
    d = jnp.maximum(dist, 1).astype(F32)
    large = max_exact + (jnp.log(d / max_exact) / math.log(REL_MAX_DIST / max_exact)
                         * (REL_BUCKETS - max_exact)).astype(jnp.int32)
    large = jnp.minimum(large, REL_BUCKETS - 1)
    return jnp.where(dist < max_exact, dist, large)


def _attn_bias_tiles(rel_bias, t):
    n = REL_MAX_DIST
    p = 2 * n
    nb = t // n
    table = rel_bias.astype(F32)[_t5_bucket(jnp.arange(p))]
    table = ((table - table[n][None]) * LOG2E).transpose(1, 2, 0)
    rot = jnp.tile(table, (1, 1, n + 1))[:, :, :n * (p + 1)].reshape(*table.shape[:2], n, p + 1)
    band0 = jnp.flip(rot[..., p - n + 1:p + 1], axis=-1)
    band1 = jnp.flip(rot[..., 1:n + 1], axis=-1)
    r = jnp.arange(t)[:, None]
    c = jnp.arange(t)[None, :]
    rb, cb = r // n, c // n
    band0, band1 = jnp.tile(band0, (1, 1, nb, nb)), jnp.tile(band1, (1, 1, nb, nb))
    diag = jnp.where(r < c, NEG_BIG,
                     jnp.where(rb == cb, band0, jnp.where(rb == cb + 1, band1, 0.0)))
    sub = jnp.where((rb == 0) & (cb == nb - 1), band1, 0.0)
    return jnp.concatenate([sub, diag], axis=-1)


def _diff_attention(q, kv, bias, lam_params, gain, lambda_init, *, t):
    S, D = q.shape
    hw = 2 * HEAD_DIM
    H = D // hw
    per_head = dict(pipeline_mode=pl.Buffered(1))
    n_chain = 2 * ATTN_ROW_GROUPS
    return pl.pallas_call(
        functools.partial(_attn_kernel, t=t, lambda_init=lambda_init),
        grid=(H, S // t),
        in_specs=[pl.BlockSpec((t, hw), lambda h, i: (i, h)),
                  pl.BlockSpec((S, hw), lambda h, i: (0, h), **per_head),
                  pl.BlockSpec((S, hw), lambda h, i: (0, H + h), **per_head),
                  pl.BlockSpec((None, 2, t, 2 * t), lambda h, i: (h, 0, 0, 0), **per_head),
                  pl.BlockSpec((4, HEAD_DIM), lambda h, i: (0, 0)),
                  pl.BlockSpec((1, hw), lambda h, i: (0, 0))],
        out_specs=pl.BlockSpec((t, hw), lambda h, i: (i, h)),
        out_shape=jax.ShapeDtypeStruct((S, D), BF16),
        scratch_shapes=[pltpu.VMEM((2, t, 128), F32), pltpu.VMEM((2, t, 128), F32),
                        pltpu.VMEM((2, t, hw), F32),
                        pltpu.VMEM((2, n_chain, t // ATTN_ROW_GROUPS, 2 * t), BF16),
                        pltpu.VMEM((2, n_chain, t // ATTN_ROW_GROUPS, 128), F32)],
        compiler_params=_params("parallel", "arbitrary"), name="diff_attn",
    )(q, kv, kv, bias, lam_params, gain.reshape(1, hw))


def _router_kernel(h_ref, g_ref, w_ref, b_ref, xn_ref, ids_ref, wts_ref):
    x = h_ref[...]
    xn = x * lax.rsqrt(jnp.mean(x * x, axis=-1, keepdims=True) + NORM_EPS) * g_ref[...]
    xn_ref[...] = _pack_halves(xn)
    xh = xn.astype(BF16)
    xl = (xn - xh.astype(F32)).astype(BF16)
    w = w_ref[...]
    wh = w.astype(BF16)
    wl = (w - wh.astype(F32)).astype(BF16)
    logits = _dot(xh, wh) + _dot(xh, wl) + _dot(xl, wh) + b_ref[...]

    lane = lax.broadcasted_iota(jnp.int32, logits.shape, 1).astype(F32)
    first = lambda hit: jnp.min(jnp.where(hit, lane, 1e9), axis=-1, keepdims=True)
    gl = jnp.where(lane < MOE_GROUPS, logits, NEG_BIG)
    gmax = jnp.max(gl, axis=-1, keepdims=True)
    g_w = 1.0 / jnp.sum(jnp.exp(gl - gmax), axis=-1, keepdims=True)
    g_idx = first(gl == gmax)
    lo = MOE_GROUPS + MOE_PER_GROUP * g_idx
    el = jnp.where((lane >= lo) & (lane < lo + MOE_PER_GROUP), logits, NEG_BIG)
    e1 = jnp.max(el, axis=-1, keepdims=True)
    i1 = first(el == e1)
    el2 = jnp.where(lane == i1, NEG_BIG, el)
    e2 = jnp.max(el2, axis=-1, keepdims=True)
    i2 = first(el2 == e2)
    r = jnp.exp(e2 - e1)
    w1 = g_w / (1.0 + r)
    w2 = g_w * r / (1.0 + r)
    ids = jnp.where(lane == 0, i1, jnp.where(lane == 1, i2, float(MOE_GROUPS))) - MOE_GROUPS
    ids_ref[...] = ids.astype(jnp.int32)
    wts_ref[...] = jnp.where(lane == 0, w1, jnp.where(lane == 1, w2, 0.0))


def _router(h, gain, w_group, b_group, w_expert, b_expert, *, tr=256):
    S, D = h.shape
    n_used = MOE_GROUPS + MOE_EXPERTS
    w = jnp.pad(jnp.concatenate([w_group, w_expert], axis=1).astype(F32), ((0, 0), (0, 128 - n_used)))
    b = jnp.pad(jnp.concatenate([b_group.reshape(-1), b_expert.reshape(-1)]).astype(F32),
                (0, 128 - n_used)).reshape(1, 128)
    row = lambda width: pl.BlockSpec((tr, width), lambda i: (i, 0))
    full = lambda shape: pl.BlockSpec(shape, lambda i: (0, 0))
    xn, ids, wts = pl.pallas_call(
        _router_kernel, grid=(S // tr,),
        in_specs=[row(D), full((1, D)), full((D, 128)), full((1, 128))],
        out_specs=[row(D // 2), row(128), row(128)],
        out_shape=[jax.ShapeDtypeStruct((S, D // 2), jnp.uint32), jax.ShapeDtypeStruct((S, 128), jnp.int32),
                   jax.ShapeDtypeStruct((S, 128), F32)],
        compiler_params=_params("parallel"), name="moe_router",
    )(h, gain.reshape(1, D), w, b)
    return xn, ids[:, :MOE_TOPK], wts[:, :MOE_TOPK]


def _dispatch(expert, weight, bm, n_tok):
    n_asg = n_tok * MOE_TOPK
    E = MOE_EXPERTS
    flat_e = expert.reshape(n_asg)
    onehot = (flat_e[:, None] == jnp.arange(E, dtype=jnp.int32)[None, :]).astype(jnp.int32)
    counts = jnp.sum(onehot, axis=0)
    rank = jnp.sum(jnp.cumsum(onehot, axis=0) * onehot, axis=1) - 1
    padded = (counts + bm - 1) // bm * bm
    pad_end = jnp.cumsum(padded)
    pad_start = pad_end - padded
    dest = jnp.sum(onehot * pad_start[None, :], axis=1) + rank
    n_blocks = -(-n_asg // bm) + E
    n_slots = n_blocks * bm
    n = jnp.arange(n_asg, dtype=jnp.int32)
    tok = n // MOE_TOPK
    row = (n % MOE_TOPK) * n_tok + tok
    w_bits = lax.bitcast_convert_type(weight.reshape(n_asg).astype(F32), jnp.int32)
    s = jnp.arange(n_slots, dtype=jnp.int32)
    spare = MOE_TOPK * n_tok + ((s // bm) % 2) * bm + s % bm
    empty = jnp.stack([jnp.zeros_like(s), spare, jnp.zeros_like(s)], axis=1)
    slots = empty.at[dest].set(jnp.stack([tok, row, w_bits], axis=1))
    block_start = jnp.arange(n_blocks, dtype=jnp.int32) * bm
    before = (pad_end[None, :] <= block_start[:, None]).astype(jnp.int32)
    block_e = jnp.minimum(jnp.sum(before, axis=1), E - 1)
    live = (block_start < pad_end[-1]).astype(jnp.int32)
    slot_w = lax.bitcast_convert_type(slots[:, 2], F32)
    return (block_e.astype(jnp.int32), live, slots[:, 0].reshape(n_blocks, 1, bm),
            slots[:, 1].reshape(n_blocks, 1, bm), slot_w.reshape(n_slots, 1), n_blocks)


def _expert_kernel(be_ref, live_ref, src_ref, nsrc_ref, pdst_ref, w_ref, x_hbm, wg_ref, wu_ref, wd_ref,
                   y_hbm, xbuf, ybuf, gsem, ssem, *, bm, n_rows):
    b = pl.program_id(0)
    live = live_ref[b] == 1
    prev_live = (b >= 1) & (live_ref[jnp.maximum(b - 1, 0)] == 1)
    ff = wd_ref.shape[0]
    lane = 128

    def gather_rows(idx_ref, buf_slot):
        def start(r):
            pltpu.make_async_copy(x_hbm.at[pl.ds(idx_ref[0, 0, r], 1), :],
                                  xbuf.at[buf_slot, pl.ds(r, 1), :], gsem.at[buf_slot]).start()
        return [functools.partial(start, r) for r in range(bm)]

    def scatter_rows(idx_ref, buf_slot):
        def start(r):
            pltpu.make_async_copy(ybuf.at[buf_slot, pl.ds(r, 1), :],
                                  y_hbm.at[pl.ds(idx_ref[0, 0, r], 1), :], ssem.at[buf_slot]).start()
        return [functools.partial(start, r) for r in range(bm)]

    def wait_block(sem, buf):
        pltpu.make_async_copy(buf, buf, sem).wait()

    def ffn(slot, starts):
        if ff % (2 * lane) == 0:
            w_cols = [lambda: wg_ref[...].astype(BF16), lambda: wu_ref[...].astype(BF16)]
        else:
            w_cols = [lambda: wg_ref[:, :ff - lane].astype(BF16),
                      lambda: jnp.concatenate([wg_ref[:, ff - lane:], wu_ref[:, :lane]], axis=1).astype(BF16),
                      lambda: wu_ref[:, lane:].astype(BF16)]
        n_dots = EXPERT_ROW_PARTS * (len(w_cols) + 1)
        share = -(-len(starts) // n_dots)
        starts = list(starts)

        def dot_with_copies(a, w):
            for start in starts[:share]:
                start()
            del starts[:share]
            return _dot(a, w())

        part = bm // EXPERT_ROW_PARTS
        for i in range(EXPERT_ROW_PARTS):
            rows = pl.ds(i * part, part)
            xb = _unpack_halves(xbuf[slot, rows, :]).astype(BF16)
            gu = jnp.concatenate([dot_with_copies(xb, w) for w in w_cols], axis=1)
            gate, up = gu[:, :ff], gu[:, ff:]
            hid = (gate * _sigmoid(gate)) * up
            y = dot_with_copies(hid.astype(BF16), lambda: wd_ref[...].astype(BF16))
            ybuf[slot, rows, :] = _pack_halves(y * w_ref[rows, :])
        assert not starts

    @pl.when(live & (b == 0))
    def _():
        ybuf[...] = jnp.zeros_like(ybuf)
        for p in range(2):
            spare = pltpu.make_async_copy(ybuf.at[p], y_hbm.at[pl.ds(n_rows + p * bm, bm), :], ssem.at[p])
            spare.start()
            spare.wait()
        for start in gather_rows(src_ref, 0):
            start()
        wait_block(gsem.at[0], xbuf.at[0])
        ffn(0, gather_rows(nsrc_ref, 1))

    for slot in range(2):
        @pl.when(live & (b >= 1) & (b % 2 == slot))
        def _(slot=slot):
            wait_block(gsem.at[slot], xbuf.at[slot])
            pl.when(b >= 2)(lambda: wait_block(ssem.at[slot], ybuf.at[slot]))
            ffn(slot, gather_rows(nsrc_ref, 1 - slot) + scatter_rows(pdst_ref, 1 - slot))

        @pl.when(jnp.logical_not(live) & prev_live & (b % 2 == slot))
        def _(slot=slot):
            wait_block(gsem.at[slot], xbuf.at[slot])
            pl.when(b >= 2)(lambda: wait_block(ssem.at[slot], ybuf.at[slot]))
            for start in scatter_rows(pdst_ref, 1 - slot):
                start()
            wait_block(ssem.at[1 - slot], ybuf.at[1 - slot])


def _moe_experts(xn, expert, weight, w_gate, w_up, w_down, layer, *, bm):
    S, half = xn.shape
    D = 2 * half
    FF = w_down.shape[2]
    assert FF % 128 == 0
    be, live, src, dst, sw, n_blocks = _dispatch(expert, weight, bm, S)
    live = jnp.concatenate([live, jnp.zeros((1,), jnp.int32)])
    last = n_blocks - 1
    at = lambda off: (lambda b, be, live: (jnp.clip(b + off, 0, last), 0, 0))
    smem = lambda off: pl.BlockSpec((1, 1, bm), at(off), memory_space=pltpu.SMEM)
    wspec = lambda shape: pl.BlockSpec((None, None) + shape,
                                       lambda b, be, live: (layer, be[jnp.minimum(b, last)], 0, 0))
    return pl.pallas_call(
        functools.partial(_expert_kernel, bm=bm, n_rows=MOE_TOPK * S),
        grid_spec=pltpu.PrefetchScalarGridSpec(
            num_scalar_prefetch=2, grid=(n_blocks + 1,),
            in_specs=[smem(0), smem(1), smem(-1),
                      pl.BlockSpec((bm, 1), lambda b, be, live: (jnp.minimum(b, last), 0)),
                      pl.BlockSpec(memory_space=pl.ANY),
                      wspec((D, FF)), wspec((D, FF)), wspec((FF, D))],
            out_specs=pl.BlockSpec(memory_space=pl.ANY),
            scratch_shapes=[pltpu.VMEM((2, bm, half), jnp.uint32), pltpu.VMEM((2, bm, half), jnp.uint32),
                            pltpu.SemaphoreType.DMA((2,)), pltpu.SemaphoreType.DMA((2,))]),
        out_shape=jax.ShapeDtypeStruct((MOE_TOPK * S + 2 * bm, half), jnp.uint32),
        compiler_params=_params("arbitrary"), name="moe_experts",
    )(be, live, src, src, dst, sw, xn, w_gate, w_up, w_down)


def _moe(h, gain, w_group, b_group, w_expert, b_expert, w_gate, w_up, w_down, layer, *, bm):
    xn, expert, weight = _router(h, gain, w_group, b_group, w_expert, b_expert)
    return _moe_experts(xn, expert, weight, w_gate, w_up, w_down, layer, bm=bm)


def _forward(x, ln_mix, ln_ffn, hg_w_in, hg_lower_bound, hg_out_norm, hg_w_o, kv_norm, w_kv,
             da_w_q, da_lambda_q1, da_lambda_k1, da_lambda_q2, da_lambda_k2, da_subln, da_w_o,
             rel_bias, moe_w_group, moe_b_group, moe_w_expert, moe_b_expert, moe_w_gate,
             moe_w_up, moe_w_down, final_norm, *, attn_tile, moe_block):
    B, S, D = x.shape
    assert B == 1 and ln_mix.shape[0] == 2 and hg_w_in.shape[0] == 1 and da_w_q.shape[0] == 1
    h = x.reshape(S, D)

    (xn,) = _norm(h, ln_mix[0:1], [BF16])
    proj = _matmul(xn, hg_w_in[0].astype(BF16), out_dtype=BF16, name="hg_in_proj")
    mix = _hgrn2(proj, hg_lower_bound, hg_out_norm[0])
    h = _matmul(mix, hg_w_o[0].astype(BF16), out_dtype=F32, res=h, name="hg_out_proj")
    wg, wu, wd = moe_w_gate, moe_w_up, moe_w_down
    y = _moe(h, ln_ffn[0], moe_w_group[0], moe_b_group[0], moe_w_expert[0], moe_b_expert[0],
             wg, wu, wd, 0, bm=moe_block)

    layer = 1
    lambda_init = 0.8 - 0.6 * math.exp(-0.3 * layer)
    h, xkv, xq = _norm(h, jnp.stack([kv_norm, ln_mix[1]]), [BF16, BF16], y=y, emit_sum=True)
    kv = _matmul(xkv, w_kv.astype(BF16), out_dtype=BF16, name="kv_proj")
    q = _matmul(xq, da_w_q[0].astype(BF16), out_dtype=BF16, scale=HEAD_DIM ** -0.5 * LOG2E,
                name="q_proj")
    lam_params = jnp.stack([da_lambda_q1[0], da_lambda_k1[0], da_lambda_q2[0], da_lambda_k2[0]])
    att = _diff_attention(q, kv, _attn_bias_tiles(rel_bias, attn_tile), lam_params, da_subln[0],
                          lambda_init, t=attn_tile)
    h = _matmul(att, da_w_o[0].astype(BF16), out_dtype=F32, res=h, name="da_out_proj")
    y = _moe(h, ln_ffn[1], moe_w_group[1], moe_b_group[1], moe_w_expert[1], moe_b_expert[1],
             wg, wu, wd, 1, bm=moe_block)

    (out,) = _norm(h, final_norm.reshape(1, D), [F32], y=y)
    return out.reshape(B, S, D)


def kernel(x, ln_mix, ln_ffn, hg_w_in, hg_lower_bound, hg_out_norm, hg_w_o, kv_norm, w_kv, da_w_q, da_lambda_q1, da_lambda_k1, da_lambda_q2, da_lambda_k2, da_subln, da_w_o, rel_bias, moe_w_group, moe_b_group, moe_w_expert, moe_b_expert, moe_w_gate, moe_w_up, moe_w_down, final_norm):
    return _forward(x, ln_mix, ln_ffn, hg_w_in, hg_lower_bound, hg_out_norm, hg_w_o, kv_norm, w_kv,
                    da_w_q, da_lambda_q1, da_lambda_k1, da_lambda_q2, da_lambda_k2, da_subln,
                    da_w_o, rel_bias, moe_w_group, moe_b_group, moe_w_expert, moe_b_expert,
                    moe_w_gate, moe_w_up, moe_w_down, final_norm,
                    attn_tile=min(512, x.shape[1]), moe_block=256)
```

```python
import functools
import math

import numpy as np
import jax
import jax.numpy as jnp
from jax import lax
from jax.experimental import pallas as pl
from jax.experimental.pallas import tpu as pltpu

F32 = jnp.float32
BF16 = jnp.bfloat16

NORM_EPS = 1e-6
HEAD_DIM = 128
MOE_GROUPS = 8
MOE_PER_GROUP = 8
MOE_EXPERTS = MOE_GROUPS * MOE_PER_GROUP
MOE_TOPK = 2
REL_BUCKETS = 32
REL_MAX_DIST = 128
NEG_BIG = -1e30
LOG2E = 1.4426950408889634

VMEM_LIMIT_BYTES = 56 * 1024 * 1024
NORM_ROWS = 256
MM_TILE_M = 1024
MM_TILE_N = 512
HG_CHUNK = 64
HG_TILE = 256
HG_HEADS_PER_STEP = 8
ATTN_TILE = 512
ATTN_PV_KEYS = 512
ATTN_ROW_GROUPS = 1
MOE_BLOCK = 256
EXPERT_ROW_PARTS = 2


def _params(*sem):
    return pltpu.CompilerParams(dimension_semantics=sem, vmem_limit_bytes=VMEM_LIMIT_BYTES)


def _sigmoid(x):
    return 1.0 / (1.0 + jnp.exp(-x))


def _dot(a, b):
    return jnp.dot(a, b, preferred_element_type=F32)


def _dot_nt(a, b):
    return lax.dot_general(a, b, (((1,), (1,)), ((), ())), preferred_element_type=F32)


def _dot_tn(a, b):
    return lax.dot_general(a, b, (((0,), (0,)), ((), ())), preferred_element_type=F32)


_HIGH_HALF = 0xFFFF0000


def _pack_halves(x):
    d = x.shape[1] // 2
    bits = lambda v: lax.bitcast_convert_type(v.astype(BF16).astype(F32), jnp.uint32)
    return (bits(x[:, :d]) >> 16) | (bits(x[:, d:]) & jnp.uint32(_HIGH_HALF))


def _unpack_halves(w):
    lo = lax.bitcast_convert_type(w << 16, F32)
    hi = lax.bitcast_convert_type(w & jnp.uint32(_HIGH_HALF), F32)
    return jnp.concatenate([lo, hi], axis=1)


def _norm_kernel(*refs, n_in, n_gain, emit_sum):
    in_refs = refs[:n_in]
    g_ref = refs[n_in]
    out_refs = refs[n_in + 1:]
    x = in_refs[0][...].astype(F32)
    for r in in_refs[1:]:
        x = x + _unpack_halves(r[...])
    k = 0
    if emit_sum:
        out_refs[0][...] = x
        k = 1
    xn = x * lax.rsqrt(jnp.mean(x * x, axis=-1, keepdims=True) + NORM_EPS)
    for j in range(n_gain):
        out_refs[k + j][...] = (xn * g_ref[j:j + 1, :]).astype(out_refs[k + j].dtype)


def _norm(h, gains, out_dtypes, *, y=None, emit_sum=False, tr=NORM_ROWS):
    S, D = h.shape
    n_gain = gains.shape[0]
    ins = [h]
    in_specs = [pl.BlockSpec((tr, D), lambda i: (i, 0))]
    if y is not None:
        ins += [y, y]
        in_specs += [pl.BlockSpec((tr, D // 2), lambda i: (i, 0)),
                     pl.BlockSpec((tr, D // 2), lambda i: (S // tr + i, 0))]
    ins.append(gains)
    in_specs.append(pl.BlockSpec((n_gain, D), lambda i: (0, 0)))
    out_shape = []
    if emit_sum:
        out_shape.append(jax.ShapeDtypeStruct((S, D), F32))
    out_shape += [jax.ShapeDtypeStruct((S, D), dt) for dt in out_dtypes]
    out_specs = [pl.BlockSpec((tr, D), lambda i: (i, 0)) for _ in out_shape]
    return pl.pallas_call(
        functools.partial(_norm_kernel, n_in=len(ins) - 1, n_gain=n_gain, emit_sum=emit_sum),
        grid=(S // tr,), in_specs=in_specs, out_specs=out_specs, out_shape=out_shape,
        compiler_params=_params("parallel"), name="norm",
    )(*ins)


def _mm_kernel(*refs, has_res, scale):
    a_ref, w_ref = refs[0], refs[1]
    o_ref = refs[-1]
    acc = _dot(a_ref[...], w_ref[...].astype(BF16))
    if scale is not None:
        acc = acc * scale
    if has_res:
        acc = acc + refs[2][...]
    o_ref[...] = acc.astype(o_ref.dtype)


def _matmul(a, w, *, out_dtype, layer=None, res=None, scale=None, tm=MM_TILE_M, tn=MM_TILE_N,
            name="matmul"):
    M, K = a.shape
    N = w.shape[-1]
    tm, tn = min(tm, M), min(tn, N)
    ins = [a, w]
    w_spec = (pl.BlockSpec((K, tn), lambda i, j: (0, j)) if layer is None
              else pl.BlockSpec((None, K, tn), lambda i, j: (layer, 0, j)))
    in_specs = [pl.BlockSpec((tm, K), lambda i, j: (i, 0)), w_spec]
    if res is not None:
        ins.append(res)
        in_specs.append(pl.BlockSpec((tm, tn), lambda i, j: (i, j)))
    return pl.pallas_call(
        functools.partial(_mm_kernel, has_res=res is not None, scale=scale),
        grid=(M // tm, N // tn), in_specs=in_specs,
        out_specs=pl.BlockSpec((tm, tn), lambda i, j: (i, j)),
        out_shape=jax.ShapeDtypeStruct((M, N), out_dtype),
        compiler_params=_params("parallel", "arbitrary"), name=name,
    )(*ins)


_HG_LEVELS = (32, 16, 8, 4, 2, 1)


def _hgrn_decay_matrix():
    C = HG_CHUNK
    t = np.arange(C)[:, None]
    j = np.arange(C)[None, :]
    blocks = [(j <= t), (j > t)]
    for w in _HG_LEVELS:
        r = (t // (2 * w)) * (2 * w) + w - 1
        upper = (t // w) % 2 == 1
        blocks.append(np.where(upper, (j > r) & (j <= t), (j > t) & (j <= r)))
    return np.concatenate(blocks, axis=0).astype(np.float32)


def _hgrn_kernel(q_ref, f_ref, v_ref, g_ref, lb_ref, gain_ref, m_ref, o_ref, st_ref, *, heads):
    C = HG_CHUNK
    dk = HEAD_DIM

    @pl.when(pl.program_id(1) == 0)
    def _():
        st_ref[...] = jnp.zeros_like(st_ref)

    lbr = lb_ref[...].astype(F32)
    lbe = jnp.exp(lbr - jnp.max(lbr, axis=0, keepdims=True))
    lb_all = lbe[0:1, :] / jnp.sum(lbe, axis=0, keepdims=True)
    gain = gain_ref[...].astype(F32)
    dec = m_ref[...]

    row = lax.broadcasted_iota(jnp.int32, (C, dk), 0)
    ti = lax.broadcasted_iota(jnp.int32, (C, C), 0)
    si = lax.broadcasted_iota(jnp.int32, (C, C), 1)
    upper = [((row // w) % 2) == 1 for w in _HG_LEVELS]
    same = [(ti // (2 * w)) == (si // (2 * w)) for w in _HG_LEVELS]

    units = [(slice(c * C, (c + 1) * C), hh, slice(hh * dk, (hh + 1) * dk))
             for c in range(q_ref.shape[0] // C) for hh in range(heads)]
    qf, kk, v, sums = [], [], [], []
    for sl, hh, cols in units:
        lb = lb_all[:, cols]
        q = q_ref[sl, cols].astype(F32)
        qf.append(q * _sigmoid(q))
        fg = lb + (1.0 - lb) * _sigmoid(f_ref[sl, cols].astype(F32))
        logf = jnp.log(fg)
        kk.append(1.0 - fg)
        v.append(v_ref[sl, cols].astype(BF16))
        hi = logf.astype(BF16)
        lo = (logf - hi.astype(F32)).astype(BF16)
        sums2 = _dot(dec, jnp.concatenate([hi, lo], axis=1))
        sums.append(sums2[:, :dk] + sums2[:, dk:])

    a = [jnp.where(ti == si, jnp.sum(qf[u] * kk[u], axis=-1, keepdims=True), 0.0)
         for u in range(len(units))]
    for l in range(len(_HG_LEVELS)):
        for u in range(len(units)):
            e = jnp.exp(sums[u][(2 + l) * C:(3 + l) * C])
            qw = jnp.where(upper[l], qf[u] * e, 0.0).astype(BF16)
            kw = jnp.where(upper[l], 0.0, kk[u] * e).astype(BF16)
            a[u] = a[u] + jnp.where(same[l], _dot_nt(qw, kw), 0.0)

    intra = [_dot(a[u].astype(BF16), v[u]) for u in range(len(units))]
    qb = [(qf[u] * jnp.exp(sums[u][0:C])).astype(BF16) for u in range(len(units))]
    kd = [(kk[u] * jnp.exp(sums[u][C:2 * C])).astype(BF16) for u in range(len(units))]
    st = [st_ref[hh] for hh in range(heads)]
    for u, (sl, hh, cols) in enumerate(units):
        o = intra[u] + _dot_nt(qb[u], st[hh].astype(BF16))
        st[hh] = st[hh] * jnp.exp(sums[u][C - 1:C, :]) + _dot_tn(v[u], kd[u])
        g = g_ref[sl, cols].astype(F32)
        on = o * lax.rsqrt(jnp.mean(o * o, axis=-1, keepdims=True) + NORM_EPS) * gain
        o_ref[sl, cols] = (on * (g * _sigmoid(g))).astype(o_ref.dtype)
    for hh in range(heads):
        st_ref[hh] = st[hh]


def _hgrn2(proj, lower_bound, out_gain):
    S = proj.shape[0]
    D = proj.shape[1] // 4
    H = D // HEAD_DIM
    T = min(HG_TILE, S)
    hpb = min(HG_HEADS_PER_STEP, H)
    G = H // hpb
    w = hpb * HEAD_DIM
    dec = jnp.asarray(_hgrn_decay_matrix(), BF16)
    blk = lambda off: pl.BlockSpec((T, w), lambda h, i, off=off: (i, off + h))
    return pl.pallas_call(
        functools.partial(_hgrn_kernel, heads=hpb),
        grid=(G, S // T),
        in_specs=[blk(0), blk(G), blk(2 * G), blk(3 * G),
                  pl.BlockSpec((lower_bound.shape[0], w), lambda h, i: (0, h)),
                  pl.BlockSpec((1, HEAD_DIM), lambda h, i: (0, 0)),
                  pl.BlockSpec(dec.shape, lambda h, i: (0, 0))],
        out_specs=pl.BlockSpec((T, w), lambda h, i: (i, h)),
        out_shape=jax.ShapeDtypeStruct((S, D), BF16),
        scratch_shapes=[pltpu.VMEM((hpb, HEAD_DIM, HEAD_DIM), F32)],
        compiler_params=_params("parallel", "arbitrary"), name="hgrn2",
    )(proj, proj, proj, proj, lower_bound, out_gain.reshape(1, HEAD_DIM), dec)


def _rep(x, n):
    return x if n == 1 else jnp.concatenate([x] * n, axis=1)


def _attn_kernel(q_ref, k_ref, v_ref, bias_ref, lam_ref, gain_ref, o_ref, m_ref, l_ref, acc_ref,
                 s_ref, r_ref, *, t, lambda_init):
    i = pl.program_id(1)
    dh = HEAD_DIM
    maps = range(2)
    m_ref[...] = jnp.full_like(m_ref, NEG_BIG)
    l_ref[...] = jnp.zeros_like(l_ref)
    acc_ref[...] = jnp.zeros_like(acc_ref)

    rg = t // ATTN_ROW_GROUPS
    chains = [(m, pl.ds(g * rg, rg)) for g in range(ATTN_ROW_GROUPS) for m in maps]

    def scores(j, width, bias):
        r0 = pl.multiple_of(j * t, t)
        s = [_dot_nt(q_ref[rows, m * dh:(m + 1) * dh], k_ref[pl.ds(r0, width), m * dh:(m + 1) * dh])
             for m, rows in chains]
        if bias is not None:
            s = [s[c] + bias(m, rows) for c, (m, rows) in enumerate(chains)]
        s = [x.astype(BF16) for x in s]
        return s, [jnp.max(x, axis=-1, keepdims=True).astype(F32) for x in s]

    def absorb(j, width, s, row_max):
        vt = v_ref[pl.ds(pl.multiple_of(j * t, t), width), :]
        m_prev = [m_ref[m, rows, :] for m, rows in chains]
        m_next = [jnp.maximum(m_prev[c], row_max[c]) for c in range(len(chains))]
        alpha = [jnp.exp2(m_prev[c] - m_next[c]) for c in range(len(chains))]
        kb = min(width, ATTN_PV_KEYS)
        p, pv = [[] for _ in chains], [0.0 for _ in chains]
        for k0 in range(0, width, kb):
            for c in range(len(chains)):
                pk = jnp.exp2(s[c][:, k0:k0 + kb] - _rep(m_next[c].astype(BF16), kb // 128))
                p[c].append(pk)
                pv[c] = pv[c] + _dot(pk, vt[k0:k0 + kb, :])
        for c, (m, rows) in enumerate(chains):
            blocks = [pk[:, k * 128:(k + 1) * 128] for pk in p[c] for k in range(kb // 128)]
            for _ in range(2):
                if len(blocks) % 2 == 0:
                    blocks = [blocks[k] + blocks[k + 1] for k in range(0, len(blocks), 2)]
            row_sum = jnp.sum(sum(b.astype(F32) for b in blocks), axis=-1, keepdims=True)
            l_ref[m, rows, :] = alpha[c] * l_ref[m, rows, :] + row_sum
            acc_ref[m, rows, :] = acc_ref[m, rows, :] * _rep(alpha[c], 2) + pv[c]
            m_ref[m, rows, :] = m_next[c]

    def span(j, width, bias):
        absorb(j, width, *scores(j, width, bias))

    n_far = jnp.maximum(i - 1, 0)
    n_span = n_far // 2

    def produce(k, slot):
        s, row_max = scores(2 * jnp.minimum(k, n_span - 1), 2 * t, None)
        for c in range(len(chains)):
            s_ref[slot, c] = s[c]
            r_ref[slot, c] = jnp.broadcast_to(row_max[c], (rg, 128))

    def consume(k, slot):
        absorb(2 * k, 2 * t, [s_ref[slot, c] for c in range(len(chains))],
               [r_ref[slot, c] for c in range(len(chains))])

    @pl.when(n_span >= 1)
    def _():
        produce(0, 0)

    def two_spans(it, carry):
        produce(2 * it + 1, 1)
        consume(2 * it, 0)
        produce(2 * it + 2, 0)
        consume(2 * it + 1, 1)
        return carry

    lax.fori_loop(0, n_span // 2, two_spans, 0)

    @pl.when(n_span % 2 == 1)
    def _():
        consume(n_span - 1, 0)

    @pl.when(n_far % 2 == 1)
    def _():
        span(i - 2, t, None)

    @pl.when(i == 0)
    def _():
        span(i, t, lambda m, rows: bias_ref[m, rows, t:2 * t])

    @pl.when(i >= 1)
    def _():
        span(i - 1, 2 * t, lambda m, rows: bias_ref[m, rows, :])

    lam_p = lam_ref[...].astype(F32)
    lam = (jnp.exp(jnp.sum(lam_p[0:1] * lam_p[1:2], axis=-1, keepdims=True))
           - jnp.exp(jnp.sum(lam_p[2:3] * lam_p[3:4], axis=-1, keepdims=True)) + lambda_init)
    o0 = acc_ref[0] * _rep(1.0 / l_ref[0], 2)
    o1 = acc_ref[1] * _rep(1.0 / l_ref[1], 2)
    att = o0 - lam * o1
    att = att * lax.rsqrt(jnp.mean(att * att, axis=-1, keepdims=True) + NORM_EPS)
    o_ref[...] = (att * gain_ref[...].astype(F32) * (1.0 - lambda_init)).astype(o_ref.dtype)


def _t5_bucket(dist):
    max_exact = REL_BUCKETS // 2
    d = jnp.maximum(dist, 1).astype(F32)
    large = max_exact + (jnp.log(d / max_exact) / math.log(REL_MAX_DIST / max_exact)
                         * (REL_BUCKETS - max_exact)).astype(jnp.int32)
    large = jnp.minimum(large, REL_BUCKETS - 1)
    return jnp.where(dist < max_exact, dist, large)


def _attn_bias_tiles(rel_bias, t):
    n = REL_MAX_DIST
    p = 2 * n
    nb = t // n
    table = rel_bias.astype(F32)[_t5_bucket(jnp.arange(p))]
    table = ((table - table[n][None]) * LOG2E).transpose(1, 2, 0)
    rot = jnp.tile(table, (1, 1, n + 1))[:, :, :n * (p + 1)].reshape(*table.shape[:2], n, p + 1)
    band0 = jnp.flip(rot[..., p - n + 1:p + 1], axis=-1)
    band1 = jnp.flip(rot[..., 1:n + 1], axis=-1)
    r = jnp.arange(t)[:, None]
    c = jnp.arange(t)[None, :]
    rb, cb = r // n, c // n
    band0, band1 = jnp.tile(band0, (1, 1, nb, nb)), jnp.tile(band1, (1, 1, nb, nb))
    diag = jnp.where(r < c, NEG_BIG,
                     jnp.where(rb == cb, band0, jnp.where(rb == cb + 1, band1, 0.0)))
    sub = jnp.where((rb == 0) & (cb == nb - 1), band1, 0.0)
    return jnp.concatenate([sub, diag], axis=-1)


def _diff_attention(q, kv, bias, lam_params, gain, lambda_init, *, t):
    S, D = q.shape
    hw = 2 * HEAD_DIM
    H = D // hw
    per_head = dict(pipeline_mode=pl.Buffered(1))
    n_chain = 2 * ATTN_ROW_GROUPS
    return pl.pallas_call(
        functools.partial(_attn_kernel, t=t, lambda_init=lambda_init),
        grid=(H, S // t),
        in_specs=[pl.BlockSpec((t, hw), lambda h, i: (i, h)),
                  pl.BlockSpec((S, hw), lambda h, i: (0, h), **per_head),
                  pl.BlockSpec((S, hw), lambda h, i: (0, H + h), **per_head),
                  pl.BlockSpec((None, 2, t, 2 * t), lambda h, i: (h, 0, 0, 0), **per_head),
                  pl.BlockSpec((4, HEAD_DIM), lambda h, i: (0, 0)),
                  pl.BlockSpec((1, hw), lambda h, i: (0, 0))],
        out_specs=pl.BlockSpec((t, hw), lambda h, i: (i, h)),
        out_shape=jax.ShapeDtypeStruct((S, D), BF16),
        scratch_shapes=[pltpu.VMEM((2, t, 128), F32), pltpu.VMEM((2, t, 128), F32),
                        pltpu.VMEM((2, t, hw), F32),
                        pltpu.VMEM((2, n_chain, t // ATTN_ROW_GROUPS, 2 * t), BF16),
                        pltpu.VMEM((2, n_chain, t // ATTN_ROW_GROUPS, 128), F32)],
        compiler_params=_params("parallel", "arbitrary"), name="diff_attn",
    )(q, kv, kv, bias, lam_params, gain.reshape(1, hw))


def _router_kernel(h_ref, g_ref, w_ref, b_ref, xn_ref, ids_ref, wts_ref):
    x = h_ref[...]
    xn = x * lax.rsqrt(jnp.mean(x * x, axis=-1, keepdims=True) + NORM_EPS) * g_ref[...]
    xn_ref[...] = _pack_halves(xn)
    xh = xn.astype(BF16)
    xl = (xn - xh.astype(F32)).astype(BF16)
    w = w_ref[...]
    wh = w.astype(BF16)
    wl = (w - wh.astype(F32)).astype(BF16)
    logits = _dot(xh, wh) + _dot(xh, wl) + _dot(xl, wh) + b_ref[...]

    lane = lax.broadcasted_iota(jnp.int32, logits.shape, 1).astype(F32)
    first = lambda hit: jnp.min(jnp.where(hit, lane, 1e9), axis=-1, keepdims=True)
    gl = jnp.where(lane < MOE_GROUPS, logits, NEG_BIG)
    gmax = jnp.max(gl, axis=-1, keepdims=True)
    g_w = 1.0 / jnp.sum(jnp.exp(gl - gmax), axis=-1, keepdims=True)
    g_idx = first(gl == gmax)
    lo = MOE_GROUPS + MOE_PER_GROUP * g_idx
    el = jnp.where((lane >= lo) & (lane < lo + MOE_PER_GROUP), logits, NEG_BIG)
    e1 = jnp.max(el, axis=-1, keepdims=True)
    i1 = first(el == e1)
    el2 = jnp.where(lane == i1, NEG_BIG, el)
    e2 = jnp.max(el2, axis=-1, keepdims=True)
    i2 = first(el2 == e2)
    r = jnp.exp(e2 - e1)
    w1 = g_w / (1.0 + r)
    w2 = g_w * r / (1.0 + r)
    ids = jnp.where(lane == 0, i1, jnp.where(lane == 1, i2, float(MOE_GROUPS))) - MOE_GROUPS
    ids_ref[...] = ids.astype(jnp.int32)
    wts_ref[...] = jnp.where(lane == 0, w1, jnp.where(lane == 1, w2, 0.0))


def _router(h, gain, w_group, b_group, w_expert, b_expert, *, tr=NORM_ROWS):
    S, D = h.shape
    n_used = MOE_GROUPS + MOE_EXPERTS
    w = jnp.pad(jnp.concatenate([w_group, w_expert], axis=1).astype(F32), ((0, 0), (0, 128 - n_used)))
    b = jnp.pad(jnp.concatenate([b_group.reshape(-1), b_expert.reshape(-1)]).astype(F32),
                (0, 128 - n_used)).reshape(1, 128)
    row = lambda width: pl.BlockSpec((tr, width), lambda i: (i, 0))
    full = lambda shape: pl.BlockSpec(shape, lambda i: (0, 0))
    xn, ids, wts = pl.pallas_call(
        _router_kernel, grid=(S // tr,),
        in_specs=[row(D), full((1, D)), full((D, 128)), full((1, 128))],
        out_specs=[row(D // 2), row(128), row(128)],
        out_shape=[jax.ShapeDtypeStruct((S, D // 2), jnp.uint32), jax.ShapeDtypeStruct((S, 128), jnp.int32),
                   jax.ShapeDtypeStruct((S, 128), F32)],
        compiler_params=_params("parallel"), name="moe_router",
    )(h, gain.reshape(1, D), w, b)
    return xn, ids[:, :MOE_TOPK], wts[:, :MOE_TOPK]


def _dispatch(expert, weight, bm, n_tok):
    n_asg = n_tok * MOE_TOPK
    E = MOE_EXPERTS
    flat_e = expert.reshape(n_asg)
    onehot = (flat_e[:, None] == jnp.arange(E, dtype=jnp.int32)[None, :]).astype(jnp.int32)
    counts = jnp.sum(onehot, axis=0)
    rank = jnp.sum(jnp.cumsum(onehot, axis=0) * onehot, axis=1) - 1
    padded = (counts + bm - 1) // bm * bm
    pad_end = jnp.cumsum(padded)
    pad_start = pad_end - padded
    dest = jnp.sum(onehot * pad_start[None, :], axis=1) + rank
    n_blocks = -(-n_asg // bm) + E
    n_slots = n_blocks * bm
    n = jnp.arange(n_asg, dtype=jnp.int32)
    tok = n // MOE_TOPK
    row = (n % MOE_TOPK) * n_tok + tok
    w_bits = lax.bitcast_convert_type(weight.reshape(n_asg).astype(F32), jnp.int32)
    s = jnp.arange(n_slots, dtype=jnp.int32)
    spare = MOE_TOPK * n_tok + ((s // bm) % 2) * bm + s % bm
    empty = jnp.stack([jnp.zeros_like(s), spare, jnp.zeros_like(s)], axis=1)
    slots = empty.at[dest].set(jnp.stack([tok, row, w_bits], axis=1))
    block_start = jnp.arange(n_blocks, dtype=jnp.int32) * bm
    before = (pad_end[None, :] <= block_start[:, None]).astype(jnp.int32)
    block_e = jnp.minimum(jnp.sum(before, axis=1), E - 1)
    live = (block_start < pad_end[-1]).astype(jnp.int32)
    slot_w = lax.bitcast_convert_type(slots[:, 2], F32)
    return (block_e.astype(jnp.int32), live, slots[:, 0].reshape(n_blocks, 1, bm),
            slots[:, 1].reshape(n_blocks, 1, bm), slot_w.reshape(n_slots, 1), n_blocks)


def _expert_kernel(be_ref, live_ref, src_ref, nsrc_ref, pdst_ref, w_ref, x_hbm, wg_ref, wu_ref, wd_ref,
                   y_hbm, xbuf, ybuf, gsem, ssem, *, bm, n_rows):
    b = pl.program_id(0)
    live = live_ref[b] == 1
    prev_live = (b >= 1) & (live_ref[jnp.maximum(b - 1, 0)] == 1)
    ff = wd_ref.shape[0]
    lane = 128

    def gather_rows(idx_ref, buf_slot):
        def start(r):
            pltpu.make_async_copy(x_hbm.at[pl.ds(idx_ref[0, 0, r], 1), :],
                                  xbuf.at[buf_slot, pl.ds(r, 1), :], gsem.at[buf_slot]).start()
        return [functools.partial(start, r) for r in range(bm)]

    def scatter_rows(idx_ref, buf_slot):
        def start(r):
            pltpu.make_async_copy(ybuf.at[buf_slot, pl.ds(r, 1), :],
                                  y_hbm.at[pl.ds(idx_ref[0, 0, r], 1), :], ssem.at[buf_slot]).start()
        return [functools.partial(start, r) for r in range(bm)]

    def wait_block(sem, buf):
        pltpu.make_async_copy(buf, buf, sem).wait()

    def ffn(slot, starts):
        if ff % (2 * lane) == 0:
            w_cols = [lambda: wg_ref[...].astype(BF16), lambda: wu_ref[...].astype(BF16)]
        else:
            w_cols = [lambda: wg_ref[:, :ff - lane].astype(BF16),
                      lambda: jnp.concatenate([wg_ref[:, ff - lane:], wu_ref[:, :lane]], axis=1).astype(BF16),
                      lambda: wu_ref[:, lane:].astype(BF16)]
        n_dots = EXPERT_ROW_PARTS * (len(w_cols) + 1)
        share = -(-len(starts) // n_dots)
        starts = list(starts)

        def dot_with_copies(a, w):
            for start in starts[:share]:
                start()
            del starts[:share]
            return _dot(a, w())

        part = bm // EXPERT_ROW_PARTS
        for i in range(EXPERT_ROW_PARTS):
            rows = pl.ds(i * part, part)
            xb = _unpack_halves(xbuf[slot, rows, :]).astype(BF16)
            gu = jnp.concatenate([dot_with_copies(xb, w) for w in w_cols], axis=1)
            gate, up = gu[:, :ff], gu[:, ff:]
            hid = (gate * _sigmoid(gate)) * up
            y = dot_with_copies(hid.astype(BF16), lambda: wd_ref[...].astype(BF16))
            ybuf[slot, rows, :] = _pack_halves(y * w_ref[rows, :])
        assert not starts

    @pl.when(live & (b == 0))
    def _():
        ybuf[...] = jnp.zeros_like(ybuf)
        for p in range(2):
            spare = pltpu.make_async_copy(ybuf.at[p], y_hbm.at[pl.ds(n_rows + p * bm, bm), :], ssem.at[p])
            spare.start()
            spare.wait()
        for start in gather_rows(src_ref, 0):
            start()
        wait_block(gsem.at[0], xbuf.at[0])
        ffn(0, gather_rows(nsrc_ref, 1))

    for slot in range(2):
        @pl.when(live & (b >= 1) & (b % 2 == slot))
        def _(slot=slot):
            wait_block(gsem.at[slot], xbuf.at[slot])
            pl.when(b >= 2)(lambda: wait_block(ssem.at[slot], ybuf.at[slot]))
            ffn(slot, gather_rows(nsrc_ref, 1 - slot) + scatter_rows(pdst_ref, 1 - slot))

        @pl.when(jnp.logical_not(live) & prev_live & (b % 2 == slot))
        def _(slot=slot):
            wait_block(gsem.at[slot], xbuf.at[slot])
            pl.when(b >= 2)(lambda: wait_block(ssem.at[slot], ybuf.at[slot]))
            for start in scatter_rows(pdst_ref, 1 - slot):
                start()
            wait_block(ssem.at[1 - slot], ybuf.at[1 - slot])


def _moe_experts(xn, expert, weight, w_gate, w_up, w_down, layer, *, bm):
    S, half = xn.shape
    D = 2 * half
    FF = w_down.shape[2]
    assert FF % 128 == 0
    be, live, src, dst, sw, n_blocks = _dispatch(expert, weight, bm, S)
    live = jnp.concatenate([live, jnp.zeros((1,), jnp.int32)])
    last = n_blocks - 1
    at = lambda off: (lambda b, be, live: (jnp.clip(b + off, 0, last), 0, 0))
    smem = lambda off: pl.BlockSpec((1, 1, bm), at(off), memory_space=pltpu.SMEM)
    wspec = lambda shape: pl.BlockSpec((None, None) + shape,
                                       lambda b, be, live: (layer, be[jnp.minimum(b, last)], 0, 0))
    return pl.pallas_call(
        functools.partial(_expert_kernel, bm=bm, n_rows=MOE_TOPK * S),
        grid_spec=pltpu.PrefetchScalarGridSpec(
            num_scalar_prefetch=2, grid=(n_blocks + 1,),
            in_specs=[smem(0), smem(1), smem(-1),
                      pl.BlockSpec((bm, 1), lambda b, be, live: (jnp.minimum(b, last), 0)),
                      pl.BlockSpec(memory_space=pl.ANY),
                      wspec((D, FF)), wspec((D, FF)), wspec((FF, D))],
            out_specs=pl.BlockSpec(memory_space=pl.ANY),
            scratch_shapes=[pltpu.VMEM((2, bm, half), jnp.uint32), pltpu.VMEM((2, bm, half), jnp.uint32),
                            pltpu.SemaphoreType.DMA((2,)), pltpu.SemaphoreType.DMA((2,))]),
        out_shape=jax.ShapeDtypeStruct((MOE_TOPK * S + 2 * bm, half), jnp.uint32),
        compiler_params=_params("arbitrary"), name="moe_experts",
    )(be, live, src, src, dst, sw, xn, w_gate, w_up, w_down)


def _moe(h, gain, w_group, b_group, w_expert, b_expert, w_gate, w_up, w_down, layer, *, bm):
    xn, expert, weight = _router(h, gain, w_group, b_group, w_expert, b_expert)
    return _moe_experts(xn, expert, weight, w_gate, w_up, w_down, layer, bm=bm)


def _forward(x, ln_mix, ln_ffn, hg_w_in, hg_lower_bound, hg_out_norm, hg_w_o, kv_norm, w_kv,
             da_w_q, da_lambda_q1, da_lambda_k1, da_lambda_q2, da_lambda_k2, da_subln, da_w_o,
             rel_bias, moe_w_group, moe_b_group, moe_w_expert, moe_b_expert, moe_w_gate,
             moe_w_up, moe_w_down, final_norm, *, attn_tile, moe_block):
    B, S, D = x.shape
    assert B == 1 and ln_mix.shape[0] == 2 and hg_w_in.shape[0] == 1 and da_w_q.shape[0] == 1
    assert D % (2 * HEAD_DIM) == 0 and S % attn_tile == 0 and attn_tile % REL_MAX_DIST == 0
    assert S % min(NORM_ROWS, S) == 0 and S % min(HG_TILE, S) == 0 and S % min(MM_TILE_M, S) == 0
    assert (S * MOE_TOPK) % moe_block == 0
    h = x.reshape(S, D)

    (xn,) = _norm(h, ln_mix[0:1], [BF16])
    proj = _matmul(xn, hg_w_in, layer=0, out_dtype=BF16, name="hg_in_proj")
    mix = _hgrn2(proj, hg_lower_bound, hg_out_norm[0])
    h = _matmul(mix, hg_w_o, layer=0, out_dtype=F32, res=h, name="hg_out_proj")
    wg, wu, wd = moe_w_gate, moe_w_up, moe_w_down
    y = _moe(h, ln_ffn[0], moe_w_group[0], moe_b_group[0], moe_w_expert[0], moe_b_expert[0],
             wg, wu, wd, 0, bm=moe_block)

    layer = 1
    lambda_init = 0.8 - 0.6 * math.exp(-0.3 * layer)
    h, xkv, xq = _norm(h, jnp.stack([kv_norm, ln_mix[1]]), [BF16, BF16], y=y, emit_sum=True)
    kv = _matmul(xkv, w_kv, out_dtype=BF16, name="kv_proj")
    q = _matmul(xq, da_w_q, layer=0, out_dtype=BF16, scale=HEAD_DIM ** -0.5 * LOG2E, name="q_proj")
    lam_params = jnp.stack([da_lambda_q1[0], da_lambda_k1[0], da_lambda_q2[0], da_lambda_k2[0]])
    att = _diff_attention(q, kv, _attn_bias_tiles(rel_bias, attn_tile), lam_params, da_subln[0],
                          lambda_init, t=attn_tile)
    h = _matmul(att, da_w_o, layer=0, out_dtype=F32, res=h, name="da_out_proj")
    y = _moe(h, ln_ffn[1], moe_w_group[1], moe_b_group[1], moe_w_expert[1], moe_b_expert[1],
             wg, wu, wd, 1, bm=moe_block)

    (out,) = _norm(h, final_norm.reshape(1, D), [F32], y=y)
    return out.reshape(B, S, D)


def kernel(x, ln_mix, ln_ffn, hg_w_in, hg_lower_bound, hg_out_norm, hg_w_o, kv_norm, w_kv, da_w_q, da_lambda_q1, da_lambda_k1, da_lambda_q2, da_lambda_k2, da_subln, da_w_o, rel_bias, moe_w_group, moe_b_group, moe_w_expert, moe_b_expert, moe_w_gate, moe_w_up, moe_w_down, final_norm):
    return _forward(x, ln_mix, ln_ffn, hg_w_in, hg_lower_bound, hg_out_norm, hg_w_o, kv_norm, w_kv,
                    da_w_q, da_lambda_q1, da_lambda_k1, da_lambda_q2, da_lambda_k2, da_subln,
                    da_w_o, rel_bias, moe_w_group, moe_b_group, moe_w_expert, moe_b_expert,
                    moe_w_gate, moe_w_up, moe_w_down, final_norm,
                    attn_tile=min(ATTN_TILE, x.shape[1]), moe_block=MOE_BLOCK)
```

```python
import functools
import math

import numpy as np
import jax
import jax.numpy as jnp
from jax import lax
from jax.experimental import pallas as pl
from jax.experimental.pallas import tpu as pltpu

F32 = jnp.float32
BF16 = jnp.bfloat16

NORM_EPS = 1e-6
HEAD_DIM = 128
MOE_GROUPS = 8
MOE_PER_GROUP = 8
MOE_EXPERTS = MOE_GROUPS * MOE_PER_GROUP
MOE_TOPK = 2
REL_BUCKETS = 32
REL_MAX_DIST = 128
NEG_BIG = -1e30
LOG2E = 1.4426950408889634

VMEM_LIMIT_BYTES = 56 * 1024 * 1024
NORM_ROWS = 256
MM_TILE_M = 1024
MM_TILE_N = 512
HG_CHUNK = 64
HG_TILE = 256
HG_HEADS_PER_STEP = 8
ATTN_TILE = 512
ATTN_PV_KEYS = 512
ATTN_ROW_GROUPS = 1
MOE_BLOCK = 256
EXPERT_ROW_PARTS = 2


def _params(*sem):
    return pltpu.CompilerParams(dimension_semantics=sem, vmem_limit_bytes=VMEM_LIMIT_BYTES)


def _sigmoid(x):
    return 1.0 / (1.0 + jnp.exp(-x))


def _dot(a, b):
    return jnp.dot(a, b, preferred_element_type=F32)


def _dot_nt(a, b):
    return lax.dot_general(a, b, (((1,), (1,)), ((), ())), preferred_element_type=F32)


def _dot_tn(a, b):
    return lax.dot_general(a, b, (((0,), (0,)), ((), ())), preferred_element_type=F32)


_HIGH_HALF = 0xFFFF0000


def _pack_halves(x):
    d = x.shape[1] // 2
    bits = lambda v: lax.bitcast_convert_type(v.astype(BF16).astype(F32), jnp.uint32)
    return (bits(x[:, :d]) >> 16) | (bits(x[:, d:]) & jnp.uint32(_HIGH_HALF))


def _unpack_halves(w):
    lo = lax.bitcast_convert_type(w << 16, F32)
    hi = lax.bitcast_convert_type(w & jnp.uint32(_HIGH_HALF), F32)
    return jnp.concatenate([lo, hi], axis=1)


def _norm_kernel(*refs, n_in, n_gain, emit_sum):
    in_refs = refs[:n_in]
    g_ref = refs[n_in]
    out_refs = refs[n_in + 1:]
    x = in_refs[0][...].astype(F32)
    for r in in_refs[1:]:
        x = x + _unpack_halves(r[...])
    k = 0
    if emit_sum:
        out_refs[0][...] = x
        k = 1
    xn = x * lax.rsqrt(jnp.mean(x * x, axis=-1, keepdims=True) + NORM_EPS)
    for j in range(n_gain):
        out_refs[k + j][...] = (xn * g_ref[j:j + 1, :]).astype(out_refs[k + j].dtype)


def _norm(h, gains, out_dtypes, *, y=None, emit_sum=False, tr=NORM_ROWS):
    S, D = h.shape
    n_gain = gains.shape[0]
    ins = [h]
    in_specs = [pl.BlockSpec((tr, D), lambda i: (i, 0))]
    if y is not None:
        ins += [y, y]
        in_specs += [pl.BlockSpec((tr, D // 2), lambda i: (i, 0)),
                     pl.BlockSpec((tr, D // 2), lambda i: (S // tr + i, 0))]
    ins.append(gains)
    in_specs.append(pl.BlockSpec((n_gain, D), lambda i: (0, 0)))
    out_shape = []
    if emit_sum:
        out_shape.append(jax.ShapeDtypeStruct((S, D), F32))
    out_shape += [jax.ShapeDtypeStruct((S, D), dt) for dt in out_dtypes]
    out_specs = [pl.BlockSpec((tr, D), lambda i: (i, 0)) for _ in out_shape]
    return pl.pallas_call(
        functools.partial(_norm_kernel, n_in=len(ins) - 1, n_gain=n_gain, emit_sum=emit_sum),
        grid=(S // tr,), in_specs=in_specs, out_specs=out_specs, out_shape=out_shape,
        compiler_params=_params("parallel"), name="norm",
    )(*ins)


def _mm_kernel(*refs, has_res, scale):
    a_ref, w_ref = refs[0], refs[1]
    o_ref = refs[-1]
    acc = _dot(a_ref[...], w_ref[...].astype(BF16))
    if scale is not None:
        acc = acc * scale
    if has_res:
        acc = acc + refs[2][...]
    o_ref[...] = acc.astype(o_ref.dtype)


def _matmul(a, w, *, out_dtype, layer=None, res=None, scale=None, tm=MM_TILE_M, tn=MM_TILE_N,
            name="matmul"):
    M, K = a.shape
    N = w.shape[-1]
    tm, tn = min(tm, M), min(tn, N)
    ins = [a, w]
    w_spec = (pl.BlockSpec((K, tn), lambda i, j: (0, j)) if layer is None
              else pl.BlockSpec((None, K, tn), lambda i, j: (layer, 0, j)))
    in_specs = [pl.BlockSpec((tm, K), lambda i, j: (i, 0)), w_spec]
    if res is not None:
        ins.append(res)
        in_specs.append(pl.BlockSpec((tm, tn), lambda i, j: (i, j)))
    return pl.pallas_call(
        functools.partial(_mm_kernel, has_res=res is not None, scale=scale),
        grid=(M // tm, N // tn), in_specs=in_specs,
        out_specs=pl.BlockSpec((tm, tn), lambda i, j: (i, j)),
        out_shape=jax.ShapeDtypeStruct((M, N), out_dtype),
        compiler_params=_params("parallel", "arbitrary"), name=name,
    )(*ins)


_HG_LEVELS = (32, 16, 8, 4, 2, 1)


def _hgrn_decay_matrix():
    C = HG_CHUNK
    t = np.arange(C)[:, None]
    j = np.arange(C)[None, :]
    blocks = [(j <= t), (j > t)]
    for w in _HG_LEVELS:
        r = (t // (2 * w)) * (2 * w) + w - 1
        upper = (t // w) % 2 == 1
        blocks.append(np.where(upper, (j > r) & (j <= t), (j > t) & (j <= r)))
    return np.concatenate(blocks, axis=0).astype(np.float32)


def _hgrn_kernel(q_ref, f_ref, v_ref, g_ref, lb_ref, gain_ref, m_ref, o_ref, st_ref, *, heads):
    C = HG_CHUNK
    dk = HEAD_DIM

    @pl.when(pl.program_id(1) == 0)
    def _():
        st_ref[...] = jnp.zeros_like(st_ref)

    lbr = lb_ref[...].astype(F32)
    lbe = jnp.exp(lbr - jnp.max(lbr, axis=0, keepdims=True))
    lb_all = lbe[0:1, :] / jnp.sum(lbe, axis=0, keepdims=True)
    gain = gain_ref[...].astype(F32)
    dec = m_ref[...]

    row = lax.broadcasted_iota(jnp.int32, (C, dk), 0)
    ti = lax.broadcasted_iota(jnp.int32, (C, C), 0)
    si = lax.broadcasted_iota(jnp.int32, (C, C), 1)
    upper = [((row // w) % 2) == 1 for w in _HG_LEVELS]
    same = [(ti // (2 * w)) == (si // (2 * w)) for w in _HG_LEVELS]

    units = [(slice(c * C, (c + 1) * C), hh, slice(hh * dk, (hh + 1) * dk))
             for c in range(q_ref.shape[0] // C) for hh in range(heads)]
    qf, kk, v, sums = [], [], [], []
    for sl, hh, cols in units:
        lb = lb_all[:, cols]
        q = q_ref[sl, cols].astype(F32)
        qf.append(q * _sigmoid(q))
        fg = lb + (1.0 - lb) * _sigmoid(f_ref[sl, cols].astype(F32))
        logf = jnp.log(fg)
        kk.append(1.0 - fg)
        v.append(v_ref[sl, cols].astype(BF16))
        hi = logf.astype(BF16)
        lo = (logf - hi.astype(F32)).astype(BF16)
        sums2 = _dot(dec, jnp.concatenate([hi, lo], axis=1))
        sums.append(sums2[:, :dk] + sums2[:, dk:])

    a = [jnp.where(ti == si, jnp.sum(qf[u] * kk[u], axis=-1, keepdims=True), 0.0)
         for u in range(len(units))]
    for l in range(len(_HG_LEVELS)):
        for u in range(len(units)):
            e = jnp.exp(sums[u][(2 + l) * C:(3 + l) * C])
            qw = jnp.where(upper[l], qf[u] * e, 0.0).astype(BF16)
            kw = jnp.where(upper[l], 0.0, kk[u] * e).astype(BF16)
            a[u] = a[u] + jnp.where(same[l], _dot_nt(qw, kw), 0.0)

    intra = [_dot(a[u].astype(BF16), v[u]) for u in range(len(units))]
    qb = [(qf[u] * jnp.exp(sums[u][0:C])).astype(BF16) for u in range(len(units))]
    kd = [(kk[u] * jnp.exp(sums[u][C:2 * C])).astype(BF16) for u in range(len(units))]
    st = [st_ref[hh] for hh in range(heads)]
    for u, (sl, hh, cols) in enumerate(units):
        o = intra[u] + _dot_nt(qb[u], st[hh].astype(BF16))
        st[hh] = st[hh] * jnp.exp(sums[u][C - 1:C, :]) + _dot_tn(v[u], kd[u])
        g = g_ref[sl, cols].astype(F32)
        on = o * lax.rsqrt(jnp.mean(o * o, axis=-1, keepdims=True) + NORM_EPS) * gain
        o_ref[sl, cols] = (on * (g * _sigmoid(g))).astype(o_ref.dtype)
    for hh in range(heads):
        st_ref[hh] = st[hh]


def _hgrn2(proj, lower_bound, out_gain):
    S = proj.shape[0]
    D = proj.shape[1] // 4
    H = D // HEAD_DIM
    T = min(HG_TILE, S)
    hpb = min(HG_HEADS_PER_STEP, H)
    G = H // hpb
    w = hpb * HEAD_DIM
    dec = jnp.asarray(_hgrn_decay_matrix(), BF16)
    blk = lambda off: pl.BlockSpec((T, w), lambda h, i, off=off: (i, off + h))
    return pl.pallas_call(
        functools.partial(_hgrn_kernel, heads=hpb),
        grid=(G, S // T),
        in_specs=[blk(0), blk(G), blk(2 * G), blk(3 * G),
                  pl.BlockSpec((lower_bound.shape[0], w), lambda h, i: (0, h)),
                  pl.BlockSpec((1, HEAD_DIM), lambda h, i: (0, 0)),
                  pl.BlockSpec(dec.shape, lambda h, i: (0, 0))],
        out_specs=pl.BlockSpec((T, w), lambda h, i: (i, h)),
        out_shape=jax.ShapeDtypeStruct((S, D), BF16),
        scratch_shapes=[pltpu.VMEM((hpb, HEAD_DIM, HEAD_DIM), F32)],
        compiler_params=_params("parallel", "arbitrary"), name="hgrn2",
    )(proj, proj, proj, proj, lower_bound, out_gain.reshape(1, HEAD_DIM), dec)


def _rep(x, n):
    return x if n == 1 else jnp.concatenate([x] * n, axis=1)


def _attn_kernel(q_ref, k_ref, v_ref, bias_ref, lam_ref, gain_ref, o_ref, m_ref, l_ref, acc_ref,
                 s_ref, r_ref, *, t, lambda_init):
    i = pl.program_id(1)
    dh = HEAD_DIM
    maps = range(2)
    m_ref[...] = jnp.full_like(m_ref, NEG_BIG)
    l_ref[...] = jnp.zeros_like(l_ref)
    acc_ref[...] = jnp.zeros_like(acc_ref)

    rg = t // ATTN_ROW_GROUPS
    chains = [(m, pl.ds(g * rg, rg)) for g in range(ATTN_ROW_GROUPS) for m in maps]

    def scores(j, width, bias):
        r0 = pl.multiple_of(j * t, t)
        s = [_dot_nt(q_ref[rows, m * dh:(m + 1) * dh], k_ref[pl.ds(r0, width), m * dh:(m + 1) * dh])
             for m, rows in chains]
        if bias is not None:
            s = [s[c] + bias(m, rows) for c, (m, rows) in enumerate(chains)]
        s = [x.astype(BF16) for x in s]
        return s, [jnp.max(x, axis=-1, keepdims=True).astype(F32) for x in s]

    def absorb(j, width, s, row_max):
        vt = v_ref[pl.ds(pl.multiple_of(j * t, t), width), :]
        m_prev = [m_ref[m, rows, :] for m, rows in chains]
        m_next = [jnp.maximum(m_prev[c], row_max[c]) for c in range(len(chains))]
        alpha = [jnp.exp2(m_prev[c] - m_next[c]) for c in range(len(chains))]
        kb = min(width, ATTN_PV_KEYS)
        p, pv = [[] for _ in chains], [0.0 for _ in chains]
        for k0 in range(0, width, kb):
            for c in range(len(chains)):
                pk = jnp.exp2(s[c][:, k0:k0 + kb] - _rep(m_next[c].astype(BF16), kb // 128))
                p[c].append(pk)
                pv[c] = pv[c] + _dot(pk, vt[k0:k0 + kb, :])
        for c, (m, rows) in enumerate(chains):
            blocks = [pk[:, k * 128:(k + 1) * 128] for pk in p[c] for k in range(kb // 128)]
            for _ in range(2):
                if len(blocks) % 2 == 0:
                    blocks = [blocks[k] + blocks[k + 1] for k in range(0, len(blocks), 2)]
            row_sum = jnp.sum(sum(b.astype(F32) for b in blocks), axis=-1, keepdims=True)
            l_ref[m, rows, :] = alpha[c] * l_ref[m, rows, :] + row_sum
            acc_ref[m, rows, :] = acc_ref[m, rows, :] * _rep(alpha[c], 2) + pv[c]
            m_ref[m, rows, :] = m_next[c]

    def span(j, width, bias):
        absorb(j, width, *scores(j, width, bias))

    n_far = jnp.maximum(i - 1, 0)
    n_span = n_far // 2

    def produce(k, slot):
        s, row_max = scores(2 * jnp.minimum(k, n_span - 1), 2 * t, None)
        for c in range(len(chains)):
            s_ref[slot, c] = s[c]
            r_ref[slot, c] = jnp.broadcast_to(row_max[c], (rg, 128))

    def consume(k, slot):
        absorb(2 * k, 2 * t, [s_ref[slot, c] for c in range(len(chains))],
               [r_ref[slot, c] for c in range(len(chains))])

    @pl.when(n_span >= 1)
    def _():
        produce(0, 0)

    def two_spans(it, carry):
        consume(2 * it, 0)
        produce(2 * it + 1, 1)
        consume(2 * it + 1, 1)
        produce(2 * it + 2, 0)
        return carry

    lax.fori_loop(0, n_span // 2, two_spans, 0)

    @pl.when(n_span % 2 == 1)
    def _():
        consume(n_span - 1, 0)

    @pl.when(n_far % 2 == 1)
    def _():
        span(i - 2, t, None)

    @pl.when(i == 0)
    def _():
        span(i, t, lambda m, rows: bias_ref[m, rows, t:2 * t])

    @pl.when(i >= 1)
    def _():
        span(i - 1, 2 * t, lambda m, rows: bias_ref[m, rows, :])

    lam_p = lam_ref[...].astype(F32)
    lam = (jnp.exp(jnp.sum(lam_p[0:1] * lam_p[1:2], axis=-1, keepdims=True))
           - jnp.exp(jnp.sum(lam_p[2:3] * lam_p[3:4], axis=-1, keepdims=True)) + lambda_init)
    o0 = acc_ref[0] * _rep(1.0 / l_ref[0], 2)
    o1 = acc_ref[1] * _rep(1.0 / l_ref[1], 2)
    att = o0 - lam * o1
    att = att * lax.rsqrt(jnp.mean(att * att, axis=-1, keepdims=True) + NORM_EPS)
    o_ref[...] = (att * gain_ref[...].astype(F32) * (1.0 - lambda_init)).astype(o_ref.dtype)


def _t5_bucket(dist):
    max_exact = REL_BUCKETS // 2
    d = jnp.maximum(dist, 1).astype(F32)
    large = max_exact + (jnp.log(d / max_exact) / math.log(REL_MAX_DIST / max_exact)
                         * (REL_BUCKETS - max_exact)).astype(jnp.int32)
    large = jnp.minimum(large, REL_BUCKETS - 1)
    return jnp.where(dist < max_exact, dist, large)


def _attn_bias_tiles(rel_bias, t):
    n = REL_MAX_DIST
    p = 2 * n
    nb = t // n
    table = rel_bias.astype(F32)[_t5_bucket(jnp.arange(p))]
    table = ((table - table[n][None]) * LOG2E).transpose(1, 2, 0)
    rot = jnp.tile(table, (1, 1, n + 1))[:, :, :n * (p + 1)].reshape(*table.shape[:2], n, p + 1)
    band0 = jnp.flip(rot[..., p - n + 1:p + 1], axis=-1)
    band1 = jnp.flip(rot[..., 1:n + 1], axis=-1)
    r = jnp.arange(t)[:, None]
    c = jnp.arange(t)[None, :]
    rb, cb = r // n, c // n
    band0, band1 = jnp.tile(band0, (1, 1, nb, nb)), jnp.tile(band1, (1, 1, nb, nb))
    diag = jnp.where(r < c, NEG_BIG,
                     jnp.where(rb == cb, band0, jnp.where(rb == cb + 1, band1, 0.0)))
    sub = jnp.where((rb == 0) & (cb == nb - 1), band1, 0.0)
    return jnp.concatenate([sub, diag], axis=-1)


def _diff_attention(q, kv, bias, lam_params, gain, lambda_init, *, t):
    S, D = q.shape
    hw = 2 * HEAD_DIM
    H = D // hw
    per_head = dict(pipeline_mode=pl.Buffered(1))
    n_chain = 2 * ATTN_ROW_GROUPS
    return pl.pallas_call(
        functools.partial(_attn_kernel, t=t, lambda_init=lambda_init),
        grid=(H, S // t),
        in_specs=[pl.BlockSpec((t, hw), lambda h, i: (i, h)),
                  pl.BlockSpec((S, hw), lambda h, i: (0, h), **per_head),
                  pl.BlockSpec((S, hw), lambda h, i: (0, H + h), **per_head),
                  pl.BlockSpec((None, 2, t, 2 * t), lambda h, i: (h, 0, 0, 0), **per_head),
                  pl.BlockSpec((4, HEAD_DIM), lambda h, i: (0, 0)),
                  pl.BlockSpec((1, hw), lambda h, i: (0, 0))],
        out_specs=pl.BlockSpec((t, hw), lambda h, i: (i, h)),
        out_shape=jax.ShapeDtypeStruct((S, D), BF16),
        scratch_shapes=[pltpu.VMEM((2, t, 128), F32), pltpu.VMEM((2, t, 128), F32),
                        pltpu.VMEM((2, t, hw), F32),
                        pltpu.VMEM((2, n_chain, t // ATTN_ROW_GROUPS, 2 * t), BF16),
                        pltpu.VMEM((2, n_chain, t // ATTN_ROW_GROUPS, 128), F32)],
        compiler_params=_params("parallel", "arbitrary"), name="diff_attn",
    )(q, kv, kv, bias, lam_params, gain.reshape(1, hw))


def _router_kernel(h_ref, g_ref, w_ref, b_ref, xn_ref, ids_ref, wts_ref):
    x = h_ref[...]
    xn = x * lax.rsqrt(jnp.mean(x * x, axis=-1, keepdims=True) + NORM_EPS) * g_ref[...]
    xn_ref[...] = _pack_halves(xn)
    xh = xn.astype(BF16)
    xl = (xn - xh.astype(F32)).astype(BF16)
    w = w_ref[...]
    wh = w.astype(BF16)
    wl = (w - wh.astype(F32)).astype(BF16)
    logits = _dot(xh, wh) + _dot(xh, wl) + _dot(xl, wh) + b_ref[...]

    lane = lax.broadcasted_iota(jnp.int32, logits.shape, 1).astype(F32)
    first = lambda hit: jnp.min(jnp.where(hit, lane, 1e9), axis=-1, keepdims=True)
    gl = jnp.where(lane < MOE_GROUPS, logits, NEG_BIG)
    gmax = jnp.max(gl, axis=-1, keepdims=True)
    g_w = 1.0 / jnp.sum(jnp.exp(gl - gmax), axis=-1, keepdims=True)
    g_idx = first(gl == gmax)
    lo = MOE_GROUPS + MOE_PER_GROUP * g_idx
    el = jnp.where((lane >= lo) & (lane < lo + MOE_PER_GROUP), logits, NEG_BIG)
    e1 = jnp.max(el, axis=-1, keepdims=True)
    i1 = first(el == e1)
    el2 = jnp.where(lane == i1, NEG_BIG, el)
    e2 = jnp.max(el2, axis=-1, keepdims=True)
    i2 = first(el2 == e2)
    r = jnp.exp(e2 - e1)
    w1 = g_w / (1.0 + r)
    w2 = g_w * r / (1.0 + r)
    ids = jnp.where(lane == 0, i1, jnp.where(lane == 1, i2, float(MOE_GROUPS))) - MOE_GROUPS
    ids_ref[...] = ids.astype(jnp.int32)
    wts_ref[...] = jnp.where(lane == 0, w1, jnp.where(lane == 1, w2, 0.0))


def _router(h, gain, w_group, b_group, w_expert, b_expert, *, tr=NORM_ROWS):
    S, D = h.shape
    n_used = MOE_GROUPS + MOE_EXPERTS
    w = jnp.pad(jnp.concatenate([w_group, w_expert], axis=1).astype(F32), ((0, 0), (0, 128 - n_used)))
    b = jnp.pad(jnp.concatenate([b_group.reshape(-1), b_expert.reshape(-1)]).astype(F32),
                (0, 128 - n_used)).reshape(1, 128)
    row = lambda width: pl.BlockSpec((tr, width), lambda i: (i, 0))
    full = lambda shape: pl.BlockSpec(shape, lambda i: (0, 0))
    xn, ids, wts = pl.pallas_call(
        _router_kernel, grid=(S // tr,),
        in_specs=[row(D), full((1, D)), full((D, 128)), full((1, 128))],
        out_specs=[row(D // 2), row(128), row(128)],
        out_shape=[jax.ShapeDtypeStruct((S, D // 2), jnp.uint32), jax.ShapeDtypeStruct((S, 128), jnp.int32),
                   jax.ShapeDtypeStruct((S, 128), F32)],
        compiler_params=_params("parallel"), name="moe_router",
    )(h, gain.reshape(1, D), w, b)
    return xn, ids[:, :MOE_TOPK], wts[:, :MOE_TOPK]


def _dispatch(expert, weight, bm, n_tok):
    n_asg = n_tok * MOE_TOPK
    E = MOE_EXPERTS
    flat_e = expert.reshape(n_asg)
    onehot = (flat_e[:, None] == jnp.arange(E, dtype=jnp.int32)[None, :]).astype(jnp.int32)
    counts = jnp.sum(onehot, axis=0)
    rank = jnp.sum(jnp.cumsum(onehot, axis=0) * onehot, axis=1) - 1
    padded = (counts + bm - 1) // bm * bm
    pad_end = jnp.cumsum(padded)
    pad_start = pad_end - padded
    dest = jnp.sum(onehot * pad_start[None, :], axis=1) + rank
    n_blocks = -(-n_asg // bm) + E
    n_slots = n_blocks * bm
    n = jnp.arange(n_asg, dtype=jnp.int32)
    tok = n // MOE_TOPK
    row = (n % MOE_TOPK) * n_tok + tok
    w_bits = lax.bitcast_convert_type(weight.reshape(n_asg).astype(F32), jnp.int32)
    s = jnp.arange(n_slots, dtype=jnp.int32)
    spare = MOE_TOPK * n_tok + ((s // bm) % 2) * bm + s % bm
    empty = jnp.stack([jnp.zeros_like(s), spare, jnp.zeros_like(s)], axis=1)
    slots = empty.at[dest].set(jnp.stack([tok, row, w_bits], axis=1))
    block_start = jnp.arange(n_blocks, dtype=jnp.int32) * bm
    before = (pad_end[None, :] <= block_start[:, None]).astype(jnp.int32)
    block_e = jnp.minimum(jnp.sum(before, axis=1), E - 1)
    live = (block_start < pad_end[-1]).astype(jnp.int32)
    slot_w = lax.bitcast_convert_type(slots[:, 2], F32)
    return (block_e.astype(jnp.int32), live, slots[:, 0].reshape(n_blocks, 1, bm),
            slots[:, 1].reshape(n_blocks, 1, bm), slot_w.reshape(n_slots, 1), n_blocks)


def _expert_kernel(be_ref, live_ref, src_ref, nsrc_ref, pdst_ref, w_ref, x_hbm, wg_ref, wu_ref, wd_ref,
                   y_hbm, xbuf, ybuf, gsem, ssem, *, bm, n_rows):
    b = pl.program_id(0)
    live = live_ref[b] == 1
    prev_live = (b >= 1) & (live_ref[jnp.maximum(b - 1, 0)] == 1)
    ff = wd_ref.shape[0]
    lane = 128

    def gather_rows(idx_ref, buf_slot):
        def start(r):
            pltpu.make_async_copy(x_hbm.at[pl.ds(idx_ref[0, 0, r], 1), :],
                                  xbuf.at[buf_slot, pl.ds(r, 1), :], gsem.at[buf_slot]).start()
        return [functools.partial(start, r) for r in range(bm)]

    def scatter_rows(idx_ref, buf_slot):
        def start(r):
            pltpu.make_async_copy(ybuf.at[buf_slot, pl.ds(r, 1), :],
                                  y_hbm.at[pl.ds(idx_ref[0, 0, r], 1), :], ssem.at[buf_slot]).start()
        return [functools.partial(start, r) for r in range(bm)]

    def wait_block(sem, buf):
        pltpu.make_async_copy(buf, buf, sem).wait()

    def ffn(slot, starts):
        if ff % (2 * lane) == 0:
            w_cols = [lambda: wg_ref[...].astype(BF16), lambda: wu_ref[...].astype(BF16)]
        else:
            w_cols = [lambda: wg_ref[:, :ff - lane].astype(BF16),
                      lambda: jnp.concatenate([wg_ref[:, ff - lane:], wu_ref[:, :lane]], axis=1).astype(BF16),
                      lambda: wu_ref[:, lane:].astype(BF16)]
        n_dots = EXPERT_ROW_PARTS * (len(w_cols) + 1)
        share = -(-len(starts) // n_dots)
        starts = list(starts)

        def dot_with_copies(a, w):
            for start in starts[:share]:
                start()
            del starts[:share]
            return _dot(a, w())

        part = bm // EXPERT_ROW_PARTS
        for i in range(EXPERT_ROW_PARTS):
            rows = pl.ds(i * part, part)
            xb = _unpack_halves(xbuf[slot, rows, :]).astype(BF16)
            gu = jnp.concatenate([dot_with_copies(xb, w) for w in w_cols], axis=1)
            gate, up = gu[:, :ff], gu[:, ff:]
            hid = (gate * _sigmoid(gate)) * up
            y = dot_with_copies(hid.astype(BF16), lambda: wd_ref[...].astype(BF16))
            ybuf[slot, rows, :] = _pack_halves(y * w_ref[rows, :])
        assert not starts

    @pl.when(live & (b == 0))
    def _():
        ybuf[...] = jnp.zeros_like(ybuf)
        for p in range(2):
            spare = pltpu.make_async_copy(ybuf.at[p], y_hbm.at[pl.ds(n_rows + p * bm, bm), :], ssem.at[p])
            spare.start()
            spare.wait()
        for start in gather_rows(src_ref, 0):
            start()
        wait_block(gsem.at[0], xbuf.at[0])
        ffn(0, gather_rows(nsrc_ref, 1))

    for slot in range(2):
        @pl.when(live & (b >= 1) & (b % 2 == slot))
        def _(slot=slot):
            wait_block(gsem.at[slot], xbuf.at[slot])
            pl.when(b >= 2)(lambda: wait_block(ssem.at[slot], ybuf.at[slot]))
            ffn(slot, gather_rows(nsrc_ref, 1 - slot) + scatter_rows(pdst_ref, 1 - slot))

        @pl.when(jnp.logical_not(live) & prev_live & (b % 2 == slot))
        def _(slot=slot):
            wait_block(gsem.at[slot], xbuf.at[slot])
            pl.when(b >= 2)(lambda: wait_block(ssem.at[slot], ybuf.at[slot]))
            for start in scatter_rows(pdst_ref, 1 - slot):
                start()
            wait_block(ssem.at[1 - slot], ybuf.at[1 - slot])


def _moe_experts(xn, expert, weight, w_gate, w_up, w_down, layer, *, bm):
    S, half = xn.shape
    D = 2 * half
    FF = w_down.shape[2]
    assert FF % 128 == 0
    be, live, src, dst, sw, n_blocks = _dispatch(expert, weight, bm, S)
    live = jnp.concatenate([live, jnp.zeros((1,), jnp.int32)])
    last = n_blocks - 1
    at = lambda off: (lambda b, be, live: (jnp.clip(b + off, 0, last), 0, 0))
    smem = lambda off: pl.BlockSpec((1, 1, bm), at(off), memory_space=pltpu.SMEM)
    wspec = lambda shape: pl.BlockSpec((None, None) + shape,
                                       lambda b, be, live: (layer, be[jnp.minimum(b, last)], 0, 0))
    return pl.pallas_call(
        functools.partial(_expert_kernel, bm=bm, n_rows=MOE_TOPK * S),
        grid_spec=pltpu.PrefetchScalarGridSpec(
            num_scalar_prefetch=2, grid=(n_blocks + 1,),
            in_specs=[smem(0), smem(1), smem(-1),
                      pl.BlockSpec((bm, 1), lambda b, be, live: (jnp.minimum(b, last), 0)),
                      pl.BlockSpec(memory_space=pl.ANY),
                      wspec((D, FF)), wspec((D, FF)), wspec((FF, D))],
            out_specs=pl.BlockSpec(memory_space=pl.ANY),
            scratch_shapes=[pltpu.VMEM((2, bm, half), jnp.uint32), pltpu.VMEM((2, bm, half), jnp.uint32),
                            pltpu.SemaphoreType.DMA((2,)), pltpu.SemaphoreType.DMA((2,))]),
        out_shape=jax.ShapeDtypeStruct((MOE_TOPK * S + 2 * bm, half), jnp.uint32),
        compiler_params=_params("arbitrary"), name="moe_experts",
    )(be, live, src, src, dst, sw, xn, w_gate, w_up, w_down)


def _moe(h, gain, w_group, b_group, w_expert, b_expert, w_gate, w_up, w_down, layer, *, bm):
    xn, expert, weight = _router(h, gain, w_group, b_group, w_expert, b_expert)
    return _moe_experts(xn, expert, weight, w_gate, w_up, w_down, layer, bm=bm)


def _forward(x, ln_mix, ln_ffn, hg_w_in, hg_lower_bound, hg_out_norm, hg_w_o, kv_norm, w_kv,
             da_w_q, da_lambda_q1, da_lambda_k1, da_lambda_q2, da_lambda_k2, da_subln, da_w_o,
             rel_bias, moe_w_group, moe_b_group, moe_w_expert, moe_b_expert, moe_w_gate,
             moe_w_up, moe_w_down, final_norm, *, attn_tile, moe_block):
    B, S, D = x.shape
    assert B == 1 and ln_mix.shape[0] == 2 and hg_w_in.shape[0] == 1 and da_w_q.shape[0] == 1
    assert D % (2 * HEAD_DIM) == 0 and S % attn_tile == 0 and attn_tile % REL_MAX_DIST == 0
    assert S % min(NORM_ROWS, S) == 0 and S % min(HG_TILE, S) == 0 and S % min(MM_TILE_M, S) == 0
    assert (S * MOE_TOPK) % moe_block == 0
    h = x.reshape(S, D)

    (xn,) = _norm(h, ln_mix[0:1], [BF16])
    proj = _matmul(xn, hg_w_in, layer=0, out_dtype=BF16, name="hg_in_proj")
    mix = _hgrn2(proj, hg_lower_bound, hg_out_norm[0])
    h = _matmul(mix, hg_w_o, layer=0, out_dtype=F32, res=h, name="hg_out_proj")
    wg, wu, wd = moe_w_gate, moe_w_up, moe_w_down
    y = _moe(h, ln_ffn[0], moe_w_group[0], moe_b_group[0], moe_w_expert[0], moe_b_expert[0],
             wg, wu, wd, 0, bm=moe_block)

    layer = 1
    lambda_init = 0.8 - 0.6 * math.exp(-0.3 * layer)
    h, xkv, xq = _norm(h, jnp.stack([kv_norm, ln_mix[1]]), [BF16, BF16], y=y, emit_sum=True)
    kv = _matmul(xkv, w_kv, out_dtype=BF16, name="kv_proj")
    q = _matmul(xq, da_w_q, layer=0, out_dtype=BF16, scale=HEAD_DIM ** -0.5 * LOG2E, name="q_proj")
    lam_params = jnp.stack([da_lambda_q1[0], da_lambda_k1[0], da_lambda_q2[0], da_lambda_k2[0]])
    att = _diff_attention(q, kv, _attn_bias_tiles(rel_bias, attn_tile), lam_params, da_subln[0],
                          lambda_init, t=attn_tile)
    h = _matmul(att, da_w_o, layer=0, out_dtype=F32, res=h, name="da_out_proj")
    y = _moe(h, ln_ffn[1], moe_w_group[1], moe_b_group[1], moe_w_expert[1], moe_b_expert[1],
             wg, wu, wd, 1, bm=moe_block)

    (out,) = _norm(h, final_norm.reshape(1, D), [F32], y=y)
    return out.reshape(B, S, D)


def kernel(x, ln_mix, ln_ffn, hg_w_in, hg_lower_bound, hg_out_norm, hg_w_o, kv_norm, w_kv, da_w_q, da_lambda_q1, da_lambda_k1, da_lambda_q2, da_lambda_k2, da_subln, da_w_o, rel_bias, moe_w_group, moe_b_group, moe_w_expert, moe_b_expert, moe_w_gate, moe_w_up, moe_w_down, final_norm):
    return _forward(x, ln_mix, ln_ffn, hg_w_in, hg_lower_bound, hg_out_norm, hg_w_o, kv_norm, w_kv,
                    da_w_q, da_lambda_q1, da_lambda_k1, da_lambda_q2, da_lambda_k2, da_subln,
                    da_w_o, rel_bias, moe_w_group, moe_b_group, moe_w_expert, moe_b_expert,
                    moe_w_gate, moe_w_up, moe_w_down, final_norm,
                    attn_tile=min(ATTN_TILE, x.shape[1]), moe_block=MOE_BLOCK)
```

```python
import functools
import math

import numpy as np
import jax
import jax.numpy as jnp
from jax import lax
from jax.experimental import pallas as pl
from jax.experimental.pallas import tpu as pltpu

F32 = jnp.float32
BF16 = jnp.bfloat16

NORM_EPS = 1e-6
HEAD_DIM = 128
MOE_GROUPS = 8
MOE_PER_GROUP = 8
MOE_EXPERTS = MOE_GROUPS * MOE_PER_GROUP
MOE_TOPK = 2
REL_BUCKETS = 32
REL_MAX_DIST = 128
NEG_BIG = -1e30
LOG2E = 1.4426950408889634

VMEM_LIMIT_BYTES = 56 * 1024 * 1024
NORM_ROWS = 256
MM_TILE_M = 1024
MM_TILE_N = 512
HG_CHUNK = 64
HG_TILE = 256
HG_HEADS_PER_STEP = 8
ATTN_TILE = 512
ATTN_PV_KEYS = 512
ATTN_ROW_GROUPS = 1
MOE_BLOCK = 256
EXPERT_ROW_PARTS = 2


def _params(*sem):
    return pltpu.CompilerParams(dimension_semantics=sem, vmem_limit_bytes=VMEM_LIMIT_BYTES)


def _sigmoid(x):
    return 1.0 / (1.0 + jnp.exp(-x))


def _dot(a, b):
    return jnp.dot(a, b, preferred_element_type=F32)


def _dot_nt(a, b):
    return lax.dot_general(a, b, (((1,), (1,)), ((), ())), preferred_element_type=F32)


def _dot_tn(a, b):
    return lax.dot_general(a, b, (((0,), (0,)), ((), ())), preferred_element_type=F32)


_HIGH_HALF = 0xFFFF0000


def _pack_halves(x):
    d = x.shape[1] // 2
    bits = lambda v: lax.bitcast_convert_type(v.astype(BF16).astype(F32), jnp.uint32)
    return (bits(x[:, :d]) >> 16) | (bits(x[:, d:]) & jnp.uint32(_HIGH_HALF))


def _unpack_halves(w):
    lo = lax.bitcast_convert_type(w << 16, F32)
    hi = lax.bitcast_convert_type(w & jnp.uint32(_HIGH_HALF), F32)
    return jnp.concatenate([lo, hi], axis=1)


def _norm_kernel(*refs, n_in, n_gain, emit_sum):
    in_refs = refs[:n_in]
    g_ref = refs[n_in]
    out_refs = refs[n_in + 1:]
    x = in_refs[0][...].astype(F32)
    for r in in_refs[1:]:
        x = x + _unpack_halves(r[...])
    k = 0
    if emit_sum:
        out_refs[0][...] = x
        k = 1
    xn = x * lax.rsqrt(jnp.mean(x * x, axis=-1, keepdims=True) + NORM_EPS)
    for j in range(n_gain):
        out_refs[k + j][...] = (xn * g_ref[j:j + 1, :]).astype(out_refs[k + j].dtype)


def _norm(h, gains, out_dtypes, *, y=None, emit_sum=False, tr=NORM_ROWS):
    S, D = h.shape
    n_gain = gains.shape[0]
    ins = [h]
    in_specs = [pl.BlockSpec((tr, D), lambda i: (i, 0))]
    if y is not None:
        ins += [y, y]
        in_specs += [pl.BlockSpec((tr, D // 2), lambda i: (i, 0)),
                     pl.BlockSpec((tr, D // 2), lambda i: (S // tr + i, 0))]
    ins.append(gains)
    in_specs.append(pl.BlockSpec((n_gain, D), lambda i: (0, 0)))
    out_shape = []
    if emit_sum:
        out_shape.append(jax.ShapeDtypeStruct((S, D), F32))
    out_shape += [jax.ShapeDtypeStruct((S, D), dt) for dt in out_dtypes]
    out_specs = [pl.BlockSpec((tr, D), lambda i: (i, 0)) for _ in out_shape]
    return pl.pallas_call(
        functools.partial(_norm_kernel, n_in=len(ins) - 1, n_gain=n_gain, emit_sum=emit_sum),
        grid=(S // tr,), in_specs=in_specs, out_specs=out_specs, out_shape=out_shape,
        compiler_params=_params("parallel"), name="norm",
    )(*ins)


def _mm_kernel(*refs, has_res, scale):
    a_ref, w_ref = refs[0], refs[1]
    o_ref = refs[-1]
    acc = _dot(a_ref[...], w_ref[...].astype(BF16))
    if scale is not None:
        acc = acc * scale
    if has_res:
        acc = acc + refs[2][...]
    o_ref[...] = acc.astype(o_ref.dtype)


def _matmul(a, w, *, out_dtype, layer=None, res=None, scale=None, tm=MM_TILE_M, tn=MM_TILE_N,
            name="matmul"):
    M, K = a.shape
    N = w.shape[-1]
    tm, tn = min(tm, M), min(tn, N)
    ins = [a, w]
    w_spec = (pl.BlockSpec((K, tn), lambda i, j: (0, j)) if layer is None
              else pl.BlockSpec((None, K, tn), lambda i, j: (layer, 0, j)))
    in_specs = [pl.BlockSpec((tm, K), lambda i, j: (i, 0)), w_spec]
    if res is not None:
        ins.append(res)
        in_specs.append(pl.BlockSpec((tm, tn), lambda i, j: (i, j)))
    return pl.pallas_call(
        functools.partial(_mm_kernel, has_res=res is not None, scale=scale),
        grid=(M // tm, N // tn), in_specs=in_specs,
        out_specs=pl.BlockSpec((tm, tn), lambda i, j: (i, j)),
        out_shape=jax.ShapeDtypeStruct((M, N), out_dtype),
        compiler_params=_params("parallel", "arbitrary"), name=name,
    )(*ins)


_HG_LEVELS = (32, 16, 8, 4, 2, 1)


def _hgrn_decay_matrix():
    C = HG_CHUNK
    t = np.arange(C)[:, None]
    j = np.arange(C)[None, :]
    blocks = [(j <= t), (j > t)]
    for w in _HG_LEVELS:
        r = (t // (2 * w)) * (2 * w) + w - 1
        upper = (t // w) % 2 == 1
        blocks.append(np.where(upper, (j > r) & (j <= t), (j > t) & (j <= r)))
    return np.concatenate(blocks, axis=0).astype(np.float32)


def _hgrn_kernel(q_ref, f_ref, v_ref, g_ref, lb_ref, gain_ref, m_ref, o_ref, st_ref, *, heads):
    C = HG_CHUNK
    dk = HEAD_DIM

    @pl.when(pl.program_id(1) == 0)
    def _():
        st_ref[...] = jnp.zeros_like(st_ref)

    lbr = lb_ref[...].astype(F32)
    lbe = jnp.exp(lbr - jnp.max(lbr, axis=0, keepdims=True))
    lb_all = lbe[0:1, :] / jnp.sum(lbe, axis=0, keepdims=True)
    gain = gain_ref[...].astype(F32)
    dec = m_ref[...]

    row = lax.broadcasted_iota(jnp.int32, (C, dk), 0)
    ti = lax.broadcasted_iota(jnp.int32, (C, C), 0)
    si = lax.broadcasted_iota(jnp.int32, (C, C), 1)
    upper = [((row // w) % 2) == 1 for w in _HG_LEVELS]
    same = [(ti // (2 * w)) == (si // (2 * w)) for w in _HG_LEVELS]

    units = [(slice(c * C, (c + 1) * C), hh, slice(hh * dk, (hh + 1) * dk))
             for c in range(q_ref.shape[0] // C) for hh in range(heads)]
    qf, kk, v, sums = [], [], [], []
    for sl, hh, cols in units:
        lb = lb_all[:, cols]
        q = q_ref[sl, cols].astype(F32)
        qf.append(q * _sigmoid(q))
        fg = lb + (1.0 - lb) * _sigmoid(f_ref[sl, cols].astype(F32))
        logf = jnp.log(fg)
        kk.append(1.0 - fg)
        v.append(v_ref[sl, cols].astype(BF16))
        hi = logf.astype(BF16)
        lo = (logf - hi.astype(F32)).astype(BF16)
        sums2 = _dot(dec, jnp.concatenate([hi, lo], axis=1))
        sums.append(sums2[:, :dk] + sums2[:, dk:])

    a = [jnp.where(ti == si, jnp.sum(qf[u] * kk[u], axis=-1, keepdims=True), 0.0)
         for u in range(len(units))]
    for l in range(len(_HG_LEVELS)):
        for u in range(len(units)):
            e = jnp.exp(sums[u][(2 + l) * C:(3 + l) * C])
            qw = jnp.where(upper[l], qf[u] * e, 0.0).astype(BF16)
            kw = jnp.where(upper[l], 0.0, kk[u] * e).astype(BF16)
            a[u] = a[u] + jnp.where(same[l], _dot_nt(qw, kw), 0.0)

    intra = [_dot(a[u].astype(BF16), v[u]) for u in range(len(units))]
    qb = [(qf[u] * jnp.exp(sums[u][0:C])).astype(BF16) for u in range(len(units))]
    kd = [(kk[u] * jnp.exp(sums[u][C:2 * C])).astype(BF16) for u in range(len(units))]
    st = [st_ref[hh] for hh in range(heads)]
    for u, (sl, hh, cols) in enumerate(units):
        o = intra[u] + _dot_nt(qb[u], st[hh].astype(BF16))
        st[hh] = st[hh] * jnp.exp(sums[u][C - 1:C, :]) + _dot_tn(v[u], kd[u])
        g = g_ref[sl, cols].astype(F32)
        on = o * lax.rsqrt(jnp.mean(o * o, axis=-1, keepdims=True) + NORM_EPS) * gain
        o_ref[sl, cols] = (on * (g * _sigmoid(g))).astype(o_ref.dtype)
    for hh in range(heads):
        st_ref[hh] = st[hh]


def _hgrn2(proj, lower_bound, out_gain):
    S = proj.shape[0]
    D = proj.shape[1] // 4
    H = D // HEAD_DIM
    T = min(HG_TILE, S)
    hpb = min(HG_HEADS_PER_STEP, H)
    G = H // hpb
    w = hpb * HEAD_DIM
    dec = jnp.asarray(_hgrn_decay_matrix(), BF16)
    blk = lambda off: pl.BlockSpec((T, w), lambda h, i, off=off: (i, off + h))
    return pl.pallas_call(
        functools.partial(_hgrn_kernel, heads=hpb),
        grid=(G, S // T),
        in_specs=[blk(0), blk(G), blk(2 * G), blk(3 * G),
                  pl.BlockSpec((lower_bound.shape[0], w), lambda h, i: (0, h)),
                  pl.BlockSpec((1, HEAD_DIM), lambda h, i: (0, 0)),
                  pl.BlockSpec(dec.shape, lambda h, i: (0, 0))],
        out_specs=pl.BlockSpec((T, w), lambda h, i: (i, h)),
        out_shape=jax.ShapeDtypeStruct((S, D), BF16),
        scratch_shapes=[pltpu.VMEM((hpb, HEAD_DIM, HEAD_DIM), F32)],
        compiler_params=_params("parallel", "arbitrary"), name="hgrn2",
    )(proj, proj, proj, proj, lower_bound, out_gain.reshape(1, HEAD_DIM), dec)


def _rep(x, n):
    return x if n == 1 else jnp.concatenate([x] * n, axis=1)


def _attn_kernel(q_ref, k_ref, v_ref, bias_ref, lam_ref, gain_ref, o_ref, m_ref, l_ref, acc_ref,
                 s_ref, r_ref, *, t, lambda_init):
    i = pl.program_id(1)
    dh = HEAD_DIM
    maps = range(2)
    m_ref[...] = jnp.full_like(m_ref, NEG_BIG)
    l_ref[...] = jnp.zeros_like(l_ref)
    acc_ref[...] = jnp.zeros_like(acc_ref)

    rg = t // ATTN_ROW_GROUPS
    chains = [(m, pl.ds(g * rg, rg)) for g in range(ATTN_ROW_GROUPS) for m in maps]

    def scores(j, width, bias):
        r0 = pl.multiple_of(j * t, t)
        s = [_dot_nt(q_ref[rows, m * dh:(m + 1) * dh], k_ref[pl.ds(r0, width), m * dh:(m + 1) * dh])
             for m, rows in chains]
        if bias is not None:
            s = [s[c] + bias(m, rows) for c, (m, rows) in enumerate(chains)]
        s = [x.astype(BF16) for x in s]
        return s, [jnp.max(x, axis=-1, keepdims=True).astype(F32) for x in s]

    def absorb(j, width, s, row_max):
        vt = v_ref[pl.ds(pl.multiple_of(j * t, t), width), :]
        m_prev = [m_ref[m, rows, :] for m, rows in chains]
        m_next = [jnp.maximum(m_prev[c], row_max[c]) for c in range(len(chains))]
        alpha = [jnp.exp2(m_prev[c] - m_next[c]) for c in range(len(chains))]
        kb = min(width, ATTN_PV_KEYS)
        p, pv = [[] for _ in chains], [0.0 for _ in chains]
        for k0 in range(0, width, kb):
            for c in range(len(chains)):
                pk = jnp.exp2(s[c][:, k0:k0 + kb] - _rep(m_next[c].astype(BF16), kb // 128))
                p[c].append(pk)
                pv[c] = pv[c] + _dot(pk, vt[k0:k0 + kb, :])
        for c, (m, rows) in enumerate(chains):
            blocks = [pk[:, k * 128:(k + 1) * 128] for pk in p[c] for k in range(kb // 128)]
            for _ in range(2):
                if len(blocks) % 2 == 0:
                    blocks = [blocks[k] + blocks[k + 1] for k in range(0, len(blocks), 2)]
            row_sum = jnp.sum(sum(b.astype(F32) for b in blocks), axis=-1, keepdims=True)
            l_ref[m, rows, :] = alpha[c] * l_ref[m, rows, :] + row_sum
            acc_ref[m, rows, :] = acc_ref[m, rows, :] * _rep(alpha[c], 2) + pv[c]
            m_ref[m, rows, :] = m_next[c]

    def span(j, width, bias):
        absorb(j, width, *scores(j, width, bias))

    n_far = jnp.maximum(i - 1, 0)
    n_span = n_far // 2

    def produce(k, slot):
        s, row_max = scores(2 * jnp.minimum(k, n_span - 1), 2 * t, None)
        for c in range(len(chains)):
            s_ref[slot, c] = s[c]
            r_ref[slot, c] = jnp.broadcast_to(row_max[c], (rg, 128))

    def consume(k, slot):
        absorb(2 * k, 2 * t, [s_ref[slot, c] for c in range(len(chains))],
               [r_ref[slot, c] for c in range(len(chains))])

    @pl.when(n_span >= 1)
    def _():
        produce(0, 0)

    def two_spans(it, carry):
        consume(2 * it, 0)
        produce(2 * it + 1, 1)
        produce(2 * it + 2, 0)
        consume(2 * it + 1, 1)
        return carry

    lax.fori_loop(0, n_span // 2, two_spans, 0)

    @pl.when(n_span % 2 == 1)
    def _():
        consume(n_span - 1, 0)

    @pl.when(n_far % 2 == 1)
    def _():
        span(i - 2, t, None)

    @pl.when(i == 0)
    def _():
        span(i, t, lambda m, rows: bias_ref[m, rows, t:2 * t])

    @pl.when(i >= 1)
    def _():
        span(i - 1, 2 * t, lambda m, rows: bias_ref[m, rows, :])

    lam_p = lam_ref[...].astype(F32)
    lam = (jnp.exp(jnp.sum(lam_p[0:1] * lam_p[1:2], axis=-1, keepdims=True))
           - jnp.exp(jnp.sum(lam_p[2:3] * lam_p[3:4], axis=-1, keepdims=True)) + lambda_init)
    o0 = acc_ref[0] * _rep(1.0 / l_ref[0], 2)
    o1 = acc_ref[1] * _rep(1.0 / l_ref[1], 2)
    att = o0 - lam * o1
    att = att * lax.rsqrt(jnp.mean(att * att, axis=-1, keepdims=True) + NORM_EPS)
    o_ref[...] = (att * gain_ref[...].astype(F32) * (1.0 - lambda_init)).astype(o_ref.dtype)


def _t5_bucket(dist):
    max_exact = REL_BUCKETS // 2
    d = jnp.maximum(dist, 1).astype(F32)
    large = max_exact + (jnp.log(d / max_exact) / math.log(REL_MAX_DIST / max_exact)
                         * (REL_BUCKETS - max_exact)).astype(jnp.int32)
    large = jnp.minimum(large, REL_BUCKETS - 1)
    return jnp.where(dist < max_exact, dist, large)


def _attn_bias_tiles(rel_bias, t):
    n = REL_MAX_DIST
    p = 2 * n
    nb = t // n
    table = rel_bias.astype(F32)[_t5_bucket(jnp.arange(p))]
    table = ((table - table[n][None]) * LOG2E).transpose(1, 2, 0)
    rot = jnp.tile(table, (1, 1, n + 1))[:, :, :n * (p + 1)].reshape(*table.shape[:2], n, p + 1)
    band0 = jnp.flip(rot[..., p - n + 1:p + 1], axis=-1)
    band1 = jnp.flip(rot[..., 1:n + 1], axis=-1)
    r = jnp.arange(t)[:, None]
    c = jnp.arange(t)[None, :]
    rb, cb = r // n, c // n
    band0, band1 = jnp.tile(band0, (1, 1, nb, nb)), jnp.tile(band1, (1, 1, nb, nb))
    diag = jnp.where(r < c, NEG_BIG,
                     jnp.where(rb == cb, band0, jnp.where(rb == cb + 1, band1, 0.0)))
    sub = jnp.where((rb == 0) & (cb == nb - 1), band1, 0.0)
    return jnp.concatenate([sub, diag], axis=-1)


def _diff_attention(q, kv, bias, lam_params, gain, lambda_init, *, t):
    S, D = q.shape
    hw = 2 * HEAD_DIM
    H = D // hw
    per_head = dict(pipeline_mode=pl.Buffered(1))
    n_chain = 2 * ATTN_ROW_GROUPS
    return pl.pallas_call(
        functools.partial(_attn_kernel, t=t, lambda_init=lambda_init),
        grid=(H, S // t),
        in_specs=[pl.BlockSpec((t, hw), lambda h, i: (i, h)),
                  pl.BlockSpec((S, hw), lambda h, i: (0, h), **per_head),
                  pl.BlockSpec((S, hw), lambda h, i: (0, H + h), **per_head),
                  pl.BlockSpec((None, 2, t, 2 * t), lambda h, i: (h, 0, 0, 0), **per_head),
                  pl.BlockSpec((4, HEAD_DIM), lambda h, i: (0, 0)),
                  pl.BlockSpec((1, hw), lambda h, i: (0, 0))],
        out_specs=pl.BlockSpec((t, hw), lambda h, i: (i, h)),
        out_shape=jax.ShapeDtypeStruct((S, D), BF16),
        scratch_shapes=[pltpu.VMEM((2, t, 128), F32), pltpu.VMEM((2, t, 128), F32),
                        pltpu.VMEM((2, t, hw), F32),
                        pltpu.VMEM((2, n_chain, t // ATTN_ROW_GROUPS, 2 * t), BF16),
                        pltpu.VMEM((2, n_chain, t // ATTN_ROW_GROUPS, 128), F32)],
        compiler_params=_params("parallel", "arbitrary"), name="diff_attn",
    )(q, kv, kv, bias, lam_params, gain.reshape(1, hw))


def _router_kernel(h_ref, g_ref, w_ref, b_ref, xn_ref, ids_ref, wts_ref):
    x = h_ref[...]
    xn = x * lax.rsqrt(jnp.mean(x * x, axis=-1, keepdims=True) + NORM_EPS) * g_ref[...]
    xn_ref[...] = _pack_halves(xn)
    xh = xn.astype(BF16)
    xl = (xn - xh.astype(F32)).astype(BF16)
    w = w_ref[...]
    wh = w.astype(BF16)
    wl = (w - wh.astype(F32)).astype(BF16)
    logits = _dot(xh, wh) + _dot(xh, wl) + _dot(xl, wh) + b_ref[...]

    lane = lax.broadcasted_iota(jnp.int32, logits.shape, 1).astype(F32)
    first = lambda hit: jnp.min(jnp.where(hit, lane, 1e9), axis=-1, keepdims=True)
    gl = jnp.where(lane < MOE_GROUPS, logits, NEG_BIG)
    gmax = jnp.max(gl, axis=-1, keepdims=True)
    g_w = 1.0 / jnp.sum(jnp.exp(gl - gmax), axis=-1, keepdims=True)
    g_idx = first(gl == gmax)
    lo = MOE_GROUPS + MOE_PER_GROUP * g_idx
    el = jnp.where((lane >= lo) & (lane < lo + MOE_PER_GROUP), logits, NEG_BIG)
    e1 = jnp.max(el, axis=-1, keepdims=True)
    i1 = first(el == e1)
    el2 = jnp.where(lane == i1, NEG_BIG, el)
    e2 = jnp.max(el2, axis=-1, keepdims=True)
    i2 = first(el2 == e2)
    r = jnp.exp(e2 - e1)
    w1 = g_w / (1.0 + r)
    w2 = g_w * r / (1.0 + r)
    ids = jnp.where(lane == 0, i1, jnp.where(lane == 1, i2, float(MOE_GROUPS))) - MOE_GROUPS
    ids_ref[...] = ids.astype(jnp.int32)
    wts_ref[...] = jnp.where(lane == 0, w1, jnp.where(lane == 1, w2, 0.0))


def _router(h, gain, w_group, b_group, w_expert, b_expert, *, tr=NORM_ROWS):
    S, D = h.shape
    n_used = MOE_GROUPS + MOE_EXPERTS
    w = jnp.pad(jnp.concatenate([w_group, w_expert], axis=1).astype(F32), ((0, 0), (0, 128 - n_used)))
    b = jnp.pad(jnp.concatenate([b_group.reshape(-1), b_expert.reshape(-1)]).astype(F32),
                (0, 128 - n_used)).reshape(1, 128)
    row = lambda width: pl.BlockSpec((tr, width), lambda i: (i, 0))
    full = lambda shape: pl.BlockSpec(shape, lambda i: (0, 0))
    xn, ids, wts = pl.pallas_call(
        _router_kernel, grid=(S // tr,),
        in_specs=[row(D), full((1, D)), full((D, 128)), full((1, 128))],
        out_specs=[row(D // 2), row(128), row(128)],
        out_shape=[jax.ShapeDtypeStruct((S, D // 2), jnp.uint32), jax.ShapeDtypeStruct((S, 128), jnp.int32),
                   jax.ShapeDtypeStruct((S, 128), F32)],
        compiler_params=_params("parallel"), name="moe_router",
    )(h, gain.reshape(1, D), w, b)
    return xn, ids[:, :MOE_TOPK], wts[:, :MOE_TOPK]


def _dispatch(expert, weight, bm, n_tok):
    n_asg = n_tok * MOE_TOPK
    E = MOE_EXPERTS
    flat_e = expert.reshape(n_asg)
    onehot = (flat_e[:, None] == jnp.arange(E, dtype=jnp.int32)[None, :]).astype(jnp.int32)
    counts = jnp.sum(onehot, axis=0)
    rank = jnp.sum(jnp.cumsum(onehot, axis=0) * onehot, axis=1) - 1
    padded = (counts + bm - 1) // bm * bm
    pad_end = jnp.cumsum(padded)
    pad_start = pad_end - padded
    dest = jnp.sum(onehot * pad_start[None, :], axis=1) + rank
    n_blocks = -(-n_asg // bm) + E
    n_slots = n_blocks * bm
    n = jnp.arange(n_asg, dtype=jnp.int32)
    tok = n // MOE_TOPK
    row = (n % MOE_TOPK) * n_tok + tok
    w_bits = lax.bitcast_convert_type(weight.reshape(n_asg).astype(F32), jnp.int32)
    s = jnp.arange(n_slots, dtype=jnp.int32)
    spare = MOE_TOPK * n_tok + ((s // bm) % 2) * bm + s % bm
    empty = jnp.stack([jnp.zeros_like(s), spare, jnp.zeros_like(s)], axis=1)
    slots = empty.at[dest].set(jnp.stack([tok, row, w_bits], axis=1))
    block_start = jnp.arange(n_blocks, dtype=jnp.int32) * bm
    before = (pad_end[None, :] <= block_start[:, None]).astype(jnp.int32)
    block_e = jnp.minimum(jnp.sum(before, axis=1), E - 1)
    live = (block_start < pad_end[-1]).astype(jnp.int32)
    slot_w = lax.bitcast_convert_type(slots[:, 2], F32)
    return (block_e.astype(jnp.int32), live, slots[:, 0].reshape(n_blocks, 1, bm),
            slots[:, 1].reshape(n_blocks, 1, bm), slot_w.reshape(n_slots, 1), n_blocks)


def _expert_kernel(be_ref, live_ref, src_ref, nsrc_ref, pdst_ref, w_ref, x_hbm, wg_ref, wu_ref, wd_ref,
                   y_hbm, xbuf, ybuf, gsem, ssem, *, bm, n_rows):
    b = pl.program_id(0)
    live = live_ref[b] == 1
    prev_live = (b >= 1) & (live_ref[jnp.maximum(b - 1, 0)] == 1)
    ff = wd_ref.shape[0]
    lane = 128

    def gather_rows(idx_ref, buf_slot):
        def start(r):
            pltpu.make_async_copy(x_hbm.at[pl.ds(idx_ref[0, 0, r], 1), :],
                                  xbuf.at[buf_slot, pl.ds(r, 1), :], gsem.at[buf_slot]).start()
        return [functools.partial(start, r) for r in range(bm)]

    def scatter_rows(idx_ref, buf_slot):
        def start(r):
            pltpu.make_async_copy(ybuf.at[buf_slot, pl.ds(r, 1), :],
                                  y_hbm.at[pl.ds(idx_ref[0, 0, r], 1), :], ssem.at[buf_slot]).start()
        return [functools.partial(start, r) for r in range(bm)]

    def wait_block(sem, buf):
        pltpu.make_async_copy(buf, buf, sem).wait()

    def ffn(slot, starts):
        if ff % (2 * lane) == 0:
            w_cols = [lambda: wg_ref[...].astype(BF16), lambda: wu_ref[...].astype(BF16)]
        else:
            w_cols = [lambda: wg_ref[:, :ff - lane].astype(BF16),
                      lambda: jnp.concatenate([wg_ref[:, ff - lane:], wu_ref[:, :lane]], axis=1).astype(BF16),
                      lambda: wu_ref[:, lane:].astype(BF16)]
        n_dots = EXPERT_ROW_PARTS * (len(w_cols) + 1)
        share = -(-len(starts) // n_dots)
        starts = list(starts)

        def dot_with_copies(a, w):
            for start in starts[:share]:
                start()
            del starts[:share]
            return _dot(a, w)

        part = bm // EXPERT_ROW_PARTS
        rows = [pl.ds(i * part, part) for i in range(EXPERT_ROW_PARTS)]
        xb = [_unpack_halves(xbuf[slot, r, :]).astype(BF16) for r in rows]
        gu = [[] for _ in rows]
        for w in w_cols:
            wb = w()
            for i in range(len(rows)):
                gu[i].append(dot_with_copies(xb[i], wb))
        hid = []
        for i in range(len(rows)):
            g = jnp.concatenate(gu[i], axis=1)
            gate, up = g[:, :ff], g[:, ff:]
            hid.append(((gate * _sigmoid(gate)) * up).astype(BF16))
        wdb = wd_ref[...].astype(BF16)
        for i, r in enumerate(rows):
            ybuf[slot, r, :] = _pack_halves(dot_with_copies(hid[i], wdb) * w_ref[r, :])
        assert not starts

    @pl.when(live & (b == 0))
    def _():
        ybuf[...] = jnp.zeros_like(ybuf)
        for p in range(2):
            spare = pltpu.make_async_copy(ybuf.at[p], y_hbm.at[pl.ds(n_rows + p * bm, bm), :], ssem.at[p])
            spare.start()
            spare.wait()
        for start in gather_rows(src_ref, 0):
            start()
        wait_block(gsem.at[0], xbuf.at[0])
        ffn(0, gather_rows(nsrc_ref, 1))

    for slot in range(2):
        @pl.when(live & (b >= 1) & (b % 2 == slot))
        def _(slot=slot):
            wait_block(gsem.at[slot], xbuf.at[slot])
            pl.when(b >= 2)(lambda: wait_block(ssem.at[slot], ybuf.at[slot]))
            ffn(slot, gather_rows(nsrc_ref, 1 - slot) + scatter_rows(pdst_ref, 1 - slot))

        @pl.when(jnp.logical_not(live) & prev_live & (b % 2 == slot))
        def _(slot=slot):
            wait_block(gsem.at[slot], xbuf.at[slot])
            pl.when(b >= 2)(lambda: wait_block(ssem.at[slot], ybuf.at[slot]))
            for start in scatter_rows(pdst_ref, 1 - slot):
                start()
            wait_block(ssem.at[1 - slot], ybuf.at[1 - slot])


def _moe_experts(xn, expert, weight, w_gate, w_up, w_down, layer, *, bm):
    S, half = xn.shape
    D = 2 * half
    FF = w_down.shape[2]
    assert FF % 128 == 0
    be, live, src, dst, sw, n_blocks = _dispatch(expert, weight, bm, S)
    live = jnp.concatenate([live, jnp.zeros((1,), jnp.int32)])
    last = n_blocks - 1
    at = lambda off: (lambda b, be, live: (jnp.clip(b + off, 0, last), 0, 0))
    smem = lambda off: pl.BlockSpec((1, 1, bm), at(off), memory_space=pltpu.SMEM)
    wspec = lambda shape: pl.BlockSpec((None, None) + shape,
                                       lambda b, be, live: (layer, be[jnp.minimum(b, last)], 0, 0))
    return pl.pallas_call(
        functools.partial(_expert_kernel, bm=bm, n_rows=MOE_TOPK * S),
        grid_spec=pltpu.PrefetchScalarGridSpec(
            num_scalar_prefetch=2, grid=(n_blocks + 1,),
            in_specs=[smem(0), smem(1), smem(-1),
                      pl.BlockSpec((bm, 1), lambda b, be, live: (jnp.minimum(b, last), 0)),
                      pl.BlockSpec(memory_space=pl.ANY),
                      wspec((D, FF)), wspec((D, FF)), wspec((FF, D))],
            out_specs=pl.BlockSpec(memory_space=pl.ANY),
            scratch_shapes=[pltpu.VMEM((2, bm, half), jnp.uint32), pltpu.VMEM((2, bm, half), jnp.uint32),
                            pltpu.SemaphoreType.DMA((2,)), pltpu.SemaphoreType.DMA((2,))]),
        out_shape=jax.ShapeDtypeStruct((MOE_TOPK * S + 2 * bm, half), jnp.uint32),
        compiler_params=_params("arbitrary"), name="moe_experts",
    )(be, live, src, src, dst, sw, xn, w_gate, w_up, w_down)


def _moe(h, gain, w_group, b_group, w_expert, b_expert, w_gate, w_up, w_down, layer, *, bm):
    xn, expert, weight = _router(h, gain, w_group, b_group, w_expert, b_expert)
    return _moe_experts(xn, expert, weight, w_gate, w_up, w_down, layer, bm=bm)


def _forward(x, ln_mix, ln_ffn, hg_w_in, hg_lower_bound, hg_out_norm, hg_w_o, kv_norm, w_kv,
             da_w_q, da_lambda_q1, da_lambda_k1, da_lambda_q2, da_lambda_k2, da_subln, da_w_o,
             rel_bias, moe_w_group, moe_b_group, moe_w_expert, moe_b_expert, moe_w_gate,
             moe_w_up, moe_w_down, final_norm, *, attn_tile, moe_block):
    B, S, D = x.shape
    assert B == 1 and ln_mix.shape[0] == 2 and hg_w_in.shape[0] == 1 and da_w_q.shape[0] == 1
    assert D % (2 * HEAD_DIM) == 0 and S % attn_tile == 0 and attn_tile % REL_MAX_DIST == 0
    assert S % min(NORM_ROWS, S) == 0 and S % min(HG_TILE, S) == 0 and S % min(MM_TILE_M, S) == 0
    assert (S * MOE_TOPK) % moe_block == 0
    h = x.reshape(S, D)

    (xn,) = _norm(h, ln_mix[0:1], [BF16])
    proj = _matmul(xn, hg_w_in, layer=0, out_dtype=BF16, name="hg_in_proj")
    mix = _hgrn2(proj, hg_lower_bound, hg_out_norm[0])
    h = _matmul(mix, hg_w_o, layer=0, out_dtype=F32, res=h, name="hg_out_proj")
    wg, wu, wd = moe_w_gate, moe_w_up, moe_w_down
    y = _moe(h, ln_ffn[0], moe_w_group[0], moe_b_group[0], moe_w_expert[0], moe_b_expert[0],
             wg, wu, wd, 0, bm=moe_block)

    layer = 1
    lambda_init = 0.8 - 0.6 * math.exp(-0.3 * layer)
    h, xkv, xq = _norm(h, jnp.stack([kv_norm, ln_mix[1]]), [BF16, BF16], y=y, emit_sum=True)
    kv = _matmul(xkv, w_kv, out_dtype=BF16, name="kv_proj")
    q = _matmul(xq, da_w_q, layer=0, out_dtype=BF16, scale=HEAD_DIM ** -0.5 * LOG2E, name="q_proj")
    lam_params = jnp.stack([da_lambda_q1[0], da_lambda_k1[0], da_lambda_q2[0], da_lambda_k2[0]])
    att = _diff_attention(q, kv, _attn_bias_tiles(rel_bias, attn_tile), lam_params, da_subln[0],
                          lambda_init, t=attn_tile)
    h = _matmul(att, da_w_o, layer=0, out_dtype=F32, res=h, name="da_out_proj")
    y = _moe(h, ln_ffn[1], moe_w_group[1], moe_b_group[1], moe_w_expert[1], moe_b_expert[1],
             wg, wu, wd, 1, bm=moe_block)

    (out,) = _norm(h, final_norm.reshape(1, D), [F32], y=y)
    return out.reshape(B, S, D)


def kernel(x, ln_mix, ln_ffn, hg_w_in, hg_lower_bound, hg_out_norm, hg_w_o, kv_norm, w_kv, da_w_q, da_lambda_q1, da_lambda_k1, da_lambda_q2, da_lambda_k2, da_subln, da_w_o, rel_bias, moe_w_group, moe_b_group, moe_w_expert, moe_b_expert, moe_w_gate, moe_w_up, moe_w_down, final_norm):
    return _forward(x, ln_mix, ln_ffn, hg_w_in, hg_lower_bound, hg_out_norm, hg_w_o, kv_norm, w_kv,
                    da_w_q, da_lambda_q1, da_lambda_k1, da_lambda_q2, da_lambda_k2, da_subln,
                    da_w_o, rel_bias, moe_w_group, moe_b_group, moe_w_expert, moe_b_expert,
                    moe_w_gate, moe_w_up, moe_w_down, final_norm,
                    attn_tile=min(ATTN_TILE, x.shape[1]), moe_block=MOE_BLOCK)
```

```python
import functools
import math

import numpy as np
import jax
import jax.numpy as jnp
from jax import lax
from jax.experimental import pallas as pl
from jax.experimental.pallas import tpu as pltpu

F32 = jnp.float32
BF16 = jnp.bfloat16

NORM_EPS = 1e-6
HEAD_DIM = 128
MOE_GROUPS = 8
MOE_PER_GROUP = 8
MOE_EXPERTS = MOE_GROUPS * MOE_PER_GROUP
MOE_TOPK = 2
REL_BUCKETS = 32
REL_MAX_DIST = 128
NEG_BIG = -1e30
LOG2E = 1.4426950408889634

VMEM_LIMIT_BYTES = 56 * 1024 * 1024
NORM_ROWS = 256
MM_TILE_M = 1024
MM_TILE_N = 512
HG_CHUNK = 64
HG_TILE = 256
HG_HEADS_PER_STEP = 8
ATTN_TILE = 512
ATTN_PV_KEYS = 512
ATTN_ROW_GROUPS = 1
MOE_BLOCK = 256
EXPERT_ROW_PARTS = 2


def _params(*sem):
    return pltpu.CompilerParams(dimension_semantics=sem, vmem_limit_bytes=VMEM_LIMIT_BYTES)


def _sigmoid(x):
    return 1.0 / (1.0 + jnp.exp(-x))


def _dot(a, b):
    return jnp.dot(a, b, preferred_element_type=F32)


def _dot_nt(a, b):
    return lax.dot_general(a, b, (((1,), (1,)), ((), ())), preferred_element_type=F32)


def _dot_tn(a, b):
    return lax.dot_general(a, b, (((0,), (0,)), ((), ())), preferred_element_type=F32)


_HIGH_HALF = 0xFFFF0000


def _pack_halves(x):
    d = x.shape[1] // 2
    bits = lambda v: lax.bitcast_convert_type(v.astype(BF16).astype(F32), jnp.uint32)
    return (bits(x[:, :d]) >> 16) | (bits(x[:, d:]) & jnp.uint32(_HIGH_HALF))


def _unpack_halves(w):
    lo = lax.bitcast_convert_type(w << 16, F32)
    hi = lax.bitcast_convert_type(w & jnp.uint32(_HIGH_HALF), F32)
    return jnp.concatenate([lo, hi], axis=1)


def _norm_kernel(*refs, n_in, n_gain, emit_sum):
    in_refs = refs[:n_in]
    g_ref = refs[n_in]
    out_refs = refs[n_in + 1:]
    x = in_refs[0][...].astype(F32)
    for r in in_refs[1:]:
        x = x + _unpack_halves(r[...])
    k = 0
    if emit_sum:
        out_refs[0][...] = x
        k = 1
    xn = x * lax.rsqrt(jnp.mean(x * x, axis=-1, keepdims=True) + NORM_EPS)
    for j in range(n_gain):
        out_refs[k + j][...] = (xn * g_ref[j:j + 1, :]).astype(out_refs[k + j].dtype)


def _norm(h, gains, out_dtypes, *, y=None, emit_sum=False, tr=NORM_ROWS):
    S, D = h.shape
    n_gain = gains.shape[0]
    ins = [h]
    in_specs = [pl.BlockSpec((tr, D), lambda i: (i, 0))]
    if y is not None:
        ins += [y, y]
        in_specs += [pl.BlockSpec((tr, D // 2), lambda i: (i, 0)),
                     pl.BlockSpec((tr, D // 2), lambda i: (S // tr + i, 0))]
    ins.append(gains)
    in_specs.append(pl.BlockSpec((n_gain, D), lambda i: (0, 0)))
    out_shape = []
    if emit_sum:
        out_shape.append(jax.ShapeDtypeStruct((S, D), F32))
    out_shape += [jax.ShapeDtypeStruct((S, D), dt) for dt in out_dtypes]
    out_specs = [pl.BlockSpec((tr, D), lambda i: (i, 0)) for _ in out_shape]
    return pl.pallas_call(
        functools.partial(_norm_kernel, n_in=len(ins) - 1, n_gain=n_gain, emit_sum=emit_sum),
        grid=(S // tr,), in_specs=in_specs, out_specs=out_specs, out_shape=out_shape,
        compiler_params=_params("parallel"), name="norm",
    )(*ins)


def _mm_kernel(*refs, has_res, scale):
    a_ref, w_ref = refs[0], refs[1]
    o_ref = refs[-1]
    acc = _dot(a_ref[...], w_ref[...].astype(BF16))
    if scale is not None:
        acc = acc * scale
    if has_res:
        acc = acc + refs[2][...]
    o_ref[...] = acc.astype(o_ref.dtype)


def _matmul(a, w, *, out_dtype, layer=None, res=None, scale=None, tm=MM_TILE_M, tn=MM_TILE_N,
            name="matmul"):
    M, K = a.shape
    N = w.shape[-1]
    tm, tn = min(tm, M), min(tn, N)
    ins = [a, w]
    w_spec = (pl.BlockSpec((K, tn), lambda i, j: (0, j)) if layer is None
              else pl.BlockSpec((None, K, tn), lambda i, j: (layer, 0, j)))
    in_specs = [pl.BlockSpec((tm, K), lambda i, j: (i, 0)), w_spec]
    if res is not None:
        ins.append(res)
        in_specs.append(pl.BlockSpec((tm, tn), lambda i, j: (i, j)))
    return pl.pallas_call(
        functools.partial(_mm_kernel, has_res=res is not None, scale=scale),
        grid=(M // tm, N // tn), in_specs=in_specs,
        out_specs=pl.BlockSpec((tm, tn), lambda i, j: (i, j)),
        out_shape=jax.ShapeDtypeStruct((M, N), out_dtype),
        compiler_params=_params("parallel", "arbitrary"), name=name,
    )(*ins)


_HG_LEVELS = (32, 16, 8, 4, 2, 1)


def _hgrn_decay_matrix():
    C = HG_CHUNK
    t = np.arange(C)[:, None]
    j = np.arange(C)[None, :]
    blocks = [(j <= t), (j > t)]
    for w in _HG_LEVELS:
        r = (t // (2 * w)) * (2 * w) + w - 1
        upper = (t // w) % 2 == 1
        blocks.append(np.where(upper, (j > r) & (j <= t), (j > t) & (j <= r)))
    return np.concatenate(blocks, axis=0).astype(np.float32)


def _hgrn_kernel(q_ref, f_ref, v_ref, g_ref, lb_ref, gain_ref, m_ref, o_ref, st_ref, *, heads):
    C = HG_CHUNK
    dk = HEAD_DIM

    @pl.when(pl.program_id(1) == 0)
    def _():
        st_ref[...] = jnp.zeros_like(st_ref)

    lbr = lb_ref[...].astype(F32)
    lbe = jnp.exp(lbr - jnp.max(lbr, axis=0, keepdims=True))
    lb_all = lbe[0:1, :] / jnp.sum(lbe, axis=0, keepdims=True)
    gain = gain_ref[...].astype(F32)
    dec = m_ref[...]

    row = lax.broadcasted_iota(jnp.int32, (C, dk), 0)
    ti = lax.broadcasted_iota(jnp.int32, (C, C), 0)
    si = lax.broadcasted_iota(jnp.int32, (C, C), 1)
    upper = [((row // w) % 2) == 1 for w in _HG_LEVELS]
    same = [(ti // (2 * w)) == (si // (2 * w)) for w in _HG_LEVELS]

    units = [(slice(c * C, (c + 1) * C), hh, slice(hh * dk, (hh + 1) * dk))
             for c in range(q_ref.shape[0] // C) for hh in range(heads)]
    qf, kk, v, sums = [], [], [], []
    for sl, hh, cols in units:
        lb = lb_all[:, cols]
        q = q_ref[sl, cols].astype(F32)
        qf.append(q * _sigmoid(q))
        fg = lb + (1.0 - lb) * _sigmoid(f_ref[sl, cols].astype(F32))
        logf = jnp.log(fg)
        kk.append(1.0 - fg)
        v.append(v_ref[sl, cols].astype(BF16))
        hi = logf.astype(BF16)
        lo = (logf - hi.astype(F32)).astype(BF16)
        sums2 = _dot(dec, jnp.concatenate([hi, lo], axis=1))
        sums.append(sums2[:, :dk] + sums2[:, dk:])

    a = [jnp.where(ti == si, jnp.sum(qf[u] * kk[u], axis=-1, keepdims=True), 0.0)
         for u in range(len(units))]
    for l in range(len(_HG_LEVELS)):
        for u in range(len(units)):
            e = jnp.exp(sums[u][(2 + l) * C:(3 + l) * C])
            qw = jnp.where(upper[l], qf[u] * e, 0.0).astype(BF16)
            kw = jnp.where(upper[l], 0.0, kk[u] * e).astype(BF16)
            a[u] = a[u] + jnp.where(same[l], _dot_nt(qw, kw), 0.0)

    intra = [_dot(a[u].astype(BF16), v[u]) for u in range(len(units))]
    qb = [(qf[u] * jnp.exp(sums[u][0:C])).astype(BF16) for u in range(len(units))]
    kd = [(kk[u] * jnp.exp(sums[u][C:2 * C])).astype(BF16) for u in range(len(units))]
    st = [st_ref[hh] for hh in range(heads)]
    for u, (sl, hh, cols) in enumerate(units):
        o = intra[u] + _dot_nt(qb[u], st[hh].astype(BF16))
        st[hh] = st[hh] * jnp.exp(sums[u][C - 1:C, :]) + _dot_tn(v[u], kd[u])
        g = g_ref[sl, cols].astype(F32)
        on = o * lax.rsqrt(jnp.mean(o * o, axis=-1, keepdims=True) + NORM_EPS) * gain
        o_ref[sl, cols] = (on * (g * _sigmoid(g))).astype(o_ref.dtype)
    for hh in range(heads):
        st_ref[hh] = st[hh]


def _hgrn2(proj, lower_bound, out_gain):
    S = proj.shape[0]
    D = proj.shape[1] // 4
    H = D // HEAD_DIM
    T = min(HG_TILE, S)
    hpb = min(HG_HEADS_PER_STEP, H)
    G = H // hpb
    w = hpb * HEAD_DIM
    dec = jnp.asarray(_hgrn_decay_matrix(), BF16)
    blk = lambda off: pl.BlockSpec((T, w), lambda h, i, off=off: (i, off + h))
    return pl.pallas_call(
        functools.partial(_hgrn_kernel, heads=hpb),
        grid=(G, S // T),
        in_specs=[blk(0), blk(G), blk(2 * G), blk(3 * G),
                  pl.BlockSpec((lower_bound.shape[0], w), lambda h, i: (0, h)),
                  pl.BlockSpec((1, HEAD_DIM), lambda h, i: (0, 0)),
                  pl.BlockSpec(dec.shape, lambda h, i: (0, 0))],
        out_specs=pl.BlockSpec((T, w), lambda h, i: (i, h)),
        out_shape=jax.ShapeDtypeStruct((S, D), BF16),
        scratch_shapes=[pltpu.VMEM((hpb, HEAD_DIM, HEAD_DIM), F32)],
        compiler_params=_params("parallel", "arbitrary"), name="hgrn2",
    )(proj, proj, proj, proj, lower_bound, out_gain.reshape(1, HEAD_DIM), dec)


def _rep(x, n):
    return x if n == 1 else jnp.concatenate([x] * n, axis=1)


def _attn_kernel(q_ref, k_ref, v_ref, bias_ref, lam_ref, gain_ref, o_ref, m_ref, l_ref, acc_ref,
                 s_ref, r_ref, *, t, lambda_init):
    i = pl.program_id(1)
    dh = HEAD_DIM
    maps = range(2)
    m_ref[...] = jnp.full_like(m_ref, NEG_BIG)
    l_ref[...] = jnp.zeros_like(l_ref)
    acc_ref[...] = jnp.zeros_like(acc_ref)

    rg = t // ATTN_ROW_GROUPS
    chains = [(m, pl.ds(g * rg, rg)) for g in range(ATTN_ROW_GROUPS) for m in maps]

    def scores(j, width, bias):
        r0 = pl.multiple_of(j * t, t)
        s = [_dot_nt(q_ref[rows, m * dh:(m + 1) * dh], k_ref[pl.ds(r0, width), m * dh:(m + 1) * dh])
             for m, rows in chains]
        if bias is not None:
            s = [s[c] + bias(m, rows) for c, (m, rows) in enumerate(chains)]
        s = [x.astype(BF16) for x in s]
        return s, [jnp.max(x, axis=-1, keepdims=True).astype(F32) for x in s]

    def absorb(j, width, s, row_max):
        vt = v_ref[pl.ds(pl.multiple_of(j * t, t), width), :]
        m_prev = [m_ref[m, rows, :] for m, rows in chains]
        m_next = [jnp.maximum(m_prev[c], row_max[c]) for c in range(len(chains))]
        alpha = [jnp.exp2(m_prev[c] - m_next[c]) for c in range(len(chains))]
        kb = min(width, ATTN_PV_KEYS)
        p, pv = [[] for _ in chains], [0.0 for _ in chains]
        for k0 in range(0, width, kb):
            for c in range(len(chains)):
                pk = jnp.exp2(s[c][:, k0:k0 + kb] - _rep(m_next[c].astype(BF16), kb // 128))
                p[c].append(pk)
                pv[c] = pv[c] + _dot(pk, vt[k0:k0 + kb, :])
        for c, (m, rows) in enumerate(chains):
            blocks = [pk[:, k * 128:(k + 1) * 128] for pk in p[c] for k in range(kb // 128)]
            for _ in range(2):
                if len(blocks) % 2 == 0:
                    blocks = [blocks[k] + blocks[k + 1] for k in range(0, len(blocks), 2)]
            row_sum = jnp.sum(sum(b.astype(F32) for b in blocks), axis=-1, keepdims=True)
            l_ref[m, rows, :] = alpha[c] * l_ref[m, rows, :] + row_sum
            acc_ref[m, rows, :] = acc_ref[m, rows, :] * _rep(alpha[c], 2) + pv[c]
            m_ref[m, rows, :] = m_next[c]

    def span(j, width, bias):
        absorb(j, width, *scores(j, width, bias))

    n_far = jnp.maximum(i - 1, 0)
    n_span = n_far // 2

    def produce(k, slot):
        s, row_max = scores(2 * jnp.minimum(k, n_span - 1), 2 * t, None)
        for c in range(len(chains)):
            s_ref[slot, c] = s[c]
            r_ref[slot, c] = jnp.broadcast_to(row_max[c], (rg, 128))

    def consume(k, slot):
        absorb(2 * k, 2 * t, [s_ref[slot, c] for c in range(len(chains))],
               [r_ref[slot, c] for c in range(len(chains))])

    @pl.when(n_span >= 1)
    def _():
        produce(0, 0)

    def two_spans(it, carry):
        consume(2 * it, 0)
        produce(2 * it + 1, 1)
        produce(2 * it + 2, 0)
        consume(2 * it + 1, 1)
        return carry

    lax.fori_loop(0, n_span // 2, two_spans, 0)

    @pl.when(n_span % 2 == 1)
    def _():
        consume(n_span - 1, 0)

    @pl.when(n_far % 2 == 1)
    def _():
        span(i - 2, t, None)

    @pl.when(i == 0)
    def _():
        span(i, t, lambda m, rows: bias_ref[m, rows, t:2 * t])

    @pl.when(i >= 1)
    def _():
        span(i - 1, 2 * t, lambda m, rows: bias_ref[m, rows, :])

    lam_p = lam_ref[...].astype(F32)
    lam = (jnp.exp(jnp.sum(lam_p[0:1] * lam_p[1:2], axis=-1, keepdims=True))
           - jnp.exp(jnp.sum(lam_p[2:3] * lam_p[3:4], axis=-1, keepdims=True)) + lambda_init)
    o0 = acc_ref[0] * _rep(1.0 / l_ref[0], 2)
    o1 = acc_ref[1] * _rep(1.0 / l_ref[1], 2)
    att = o0 - lam * o1
    att = att * lax.rsqrt(jnp.mean(att * att, axis=-1, keepdims=True) + NORM_EPS)
    o_ref[...] = (att * gain_ref[...].astype(F32) * (1.0 - lambda_init)).astype(o_ref.dtype)


def _t5_bucket(dist):
    max_exact = REL_BUCKETS // 2
    d = jnp.maximum(dist, 1).astype(F32)
    large = max_exact + (jnp.log(d / max_exact) / math.log(REL_MAX_DIST / max_exact)
                         * (REL_BUCKETS - max_exact)).astype(jnp.int32)
    large = jnp.minimum(large, REL_BUCKETS - 1)
    return jnp.where(dist < max_exact, dist, large)


def _attn_bias_tiles(rel_bias, t):
    n = REL_MAX_DIST
    p = 2 * n
    nb = t // n
    table = rel_bias.astype(F32)[_t5_bucket(jnp.arange(p))]
    table = ((table - table[n][None]) * LOG2E).transpose(1, 2, 0)
    rot = jnp.tile(table, (1, 1, n + 1))[:, :, :n * (p + 1)].reshape(*table.shape[:2], n, p + 1)
    band0 = jnp.flip(rot[..., p - n + 1:p + 1], axis=-1)
    band1 = jnp.flip(rot[..., 1:n + 1], axis=-1)
    r = jnp.arange(t)[:, None]
    c = jnp.arange(t)[None, :]
    rb, cb = r // n, c // n
    band0, band1 = jnp.tile(band0, (1, 1, nb, nb)), jnp.tile(band1, (1, 1, nb, nb))
    diag = jnp.where(r < c, NEG_BIG,
                     jnp.where(rb == cb, band0, jnp.where(rb == cb + 1, band1, 0.0)))
    sub = jnp.where((rb == 0) & (cb == nb - 1), band1, 0.0)
    return jnp.concatenate([sub, diag], axis=-1)


def _diff_attention(q, kv, bias, lam_params, gain, lambda_init, *, t):
    S, D = q.shape
    hw = 2 * HEAD_DIM
    H = D // hw
    per_head = dict(pipeline_mode=pl.Buffered(1))
    n_chain = 2 * ATTN_ROW_GROUPS
    return pl.pallas_call(
        functools.partial(_attn_kernel, t=t, lambda_init=lambda_init),
        grid=(H, S // t),
        in_specs=[pl.BlockSpec((t, hw), lambda h, i: (i, h)),
                  pl.BlockSpec((S, hw), lambda h, i: (0, h), **per_head),
                  pl.BlockSpec((S, hw), lambda h, i: (0, H + h), **per_head),
                  pl.BlockSpec((None, 2, t, 2 * t), lambda h, i: (h, 0, 0, 0), **per_head),
                  pl.BlockSpec((4, HEAD_DIM), lambda h, i: (0, 0)),
                  pl.BlockSpec((1, hw), lambda h, i: (0, 0))],
        out_specs=pl.BlockSpec((t, hw), lambda h, i: (i, h)),
        out_shape=jax.ShapeDtypeStruct((S, D), BF16),
        scratch_shapes=[pltpu.VMEM((2, t, 128), F32), pltpu.VMEM((2, t, 128), F32),
                        pltpu.VMEM((2, t, hw), F32),
                        pltpu.VMEM((2, n_chain, t // ATTN_ROW_GROUPS, 2 * t), BF16),
                        pltpu.VMEM((2, n_chain, t // ATTN_ROW_GROUPS, 128), F32)],
        compiler_params=_params("parallel", "arbitrary"), name="diff_attn",
    )(q, kv, kv, bias, lam_params, gain.reshape(1, hw))


def _router_kernel(h_ref, g_ref, w_ref, b_ref, xn_ref, ids_ref, wts_ref):
    x = h_ref[...]
    xn = x * lax.rsqrt(jnp.mean(x * x, axis=-1, keepdims=True) + NORM_EPS) * g_ref[...]
    xn_ref[...] = _pack_halves(xn)
    xh = xn.astype(BF16)
    xl = (xn - xh.astype(F32)).astype(BF16)
    w = w_ref[...]
    wh = w.astype(BF16)
    wl = (w - wh.astype(F32)).astype(BF16)
    logits = _dot(xh, wh) + _dot(xh, wl) + _dot(xl, wh) + b_ref[...]

    lane = lax.broadcasted_iota(jnp.int32, logits.shape, 1).astype(F32)
    first = lambda hit: jnp.min(jnp.where(hit, lane, 1e9), axis=-1, keepdims=True)
    gl = jnp.where(lane < MOE_GROUPS, logits, NEG_BIG)
    gmax = jnp.max(gl, axis=-1, keepdims=True)
    g_w = 1.0 / jnp.sum(jnp.exp(gl - gmax), axis=-1, keepdims=True)
    g_idx = first(gl == gmax)
    lo = MOE_GROUPS + MOE_PER_GROUP * g_idx
    el = jnp.where((lane >= lo) & (lane < lo + MOE_PER_GROUP), logits, NEG_BIG)
    e1 = jnp.max(el, axis=-1, keepdims=True)
    i1 = first(el == e1)
    el2 = jnp.where(lane == i1, NEG_BIG, el)
    e2 = jnp.max(el2, axis=-1, keepdims=True)
    i2 = first(el2 == e2)
    r = jnp.exp(e2 - e1)
    w1 = g_w / (1.0 + r)
    w2 = g_w * r / (1.0 + r)
    ids = jnp.where(lane == 0, i1, jnp.where(lane == 1, i2, float(MOE_GROUPS))) - MOE_GROUPS
    ids_ref[...] = ids.astype(jnp.int32)
    wts_ref[...] = jnp.where(lane == 0, w1, jnp.where(lane == 1, w2, 0.0))


def _router(h, gain, w_group, b_group, w_expert, b_expert, *, tr=NORM_ROWS):
    S, D = h.shape
    n_used = MOE_GROUPS + MOE_EXPERTS
    w = jnp.pad(jnp.concatenate([w_group, w_expert], axis=1).astype(F32), ((0, 0), (0, 128 - n_used)))
    b = jnp.pad(jnp.concatenate([b_group.reshape(-1), b_expert.reshape(-1)]).astype(F32),
                (0, 128 - n_used)).reshape(1, 128)
    row = lambda width: pl.BlockSpec((tr, width), lambda i: (i, 0))
    full = lambda shape: pl.BlockSpec(shape, lambda i: (0, 0))
    xn, ids, wts = pl.pallas_call(
        _router_kernel, grid=(S // tr,),
        in_specs=[row(D), full((1, D)), full((D, 128)), full((1, 128))],
        out_specs=[row(D // 2), row(128), row(128)],
        out_shape=[jax.ShapeDtypeStruct((S, D // 2), jnp.uint32), jax.ShapeDtypeStruct((S, 128), jnp.int32),
                   jax.ShapeDtypeStruct((S, 128), F32)],
        compiler_params=_params("parallel"), name="moe_router",
    )(h, gain.reshape(1, D), w, b)
    return xn, ids[:, :MOE_TOPK], wts[:, :MOE_TOPK]


def _dispatch(expert, weight, bm, n_tok):
    n_asg = n_tok * MOE_TOPK
    E = MOE_EXPERTS
    flat_e = expert.reshape(n_asg)
    onehot = (flat_e[:, None] == jnp.arange(E, dtype=jnp.int32)[None, :]).astype(jnp.int32)
    counts = jnp.sum(onehot, axis=0)
    rank = jnp.sum(jnp.cumsum(onehot, axis=0) * onehot, axis=1) - 1
    padded = (counts + bm - 1) // bm * bm
    pad_end = jnp.cumsum(padded)
    pad_start = pad_end - padded
    dest = jnp.sum(onehot * pad_start[None, :], axis=1) + rank
    n_blocks = -(-n_asg // bm) + E
    n_slots = n_blocks * bm
    n = jnp.arange(n_asg, dtype=jnp.int32)
    tok = n // MOE_TOPK
    row = (n % MOE_TOPK) * n_tok + tok
    w_bits = lax.bitcast_convert_type(weight.reshape(n_asg).astype(F32), jnp.int32)
    s = jnp.arange(n_slots, dtype=jnp.int32)
    spare = MOE_TOPK * n_tok + ((s // bm) % 2) * bm + s % bm
    empty = jnp.stack([jnp.zeros_like(s), spare, jnp.zeros_like(s)], axis=1)
    slots = empty.at[dest].set(jnp.stack([tok, row, w_bits], axis=1))
    block_start = jnp.arange(n_blocks, dtype=jnp.int32) * bm
    before = (pad_end[None, :] <= block_start[:, None]).astype(jnp.int32)
    block_e = jnp.minimum(jnp.sum(before, axis=1), E - 1)
    live = (block_start < pad_end[-1]).astype(jnp.int32)
    slot_w = lax.bitcast_convert_type(slots[:, 2], F32)
    return (block_e.astype(jnp.int32), live, slots[:, 0].reshape(n_blocks, 1, bm),
            slots[:, 1].reshape(n_blocks, 1, bm), slot_w.reshape(n_slots, 1), n_blocks)


def _expert_kernel(be_ref, live_ref, src_ref, nsrc_ref, pdst_ref, w_ref, x_hbm, wg_ref, wu_ref, wd_ref,
                   y_hbm, xbuf, ybuf, gsem, ssem, *, bm, n_rows):
    b = pl.program_id(0)
    live = live_ref[b] == 1
    prev_live = (b >= 1) & (live_ref[jnp.maximum(b - 1, 0)] == 1)
    ff = wd_ref.shape[0]
    lane = 128

    def gather_rows(idx_ref, buf_slot):
        def start(r):
            pltpu.make_async_copy(x_hbm.at[pl.ds(idx_ref[0, 0, r], 1), :],
                                  xbuf.at[buf_slot, pl.ds(r, 1), :], gsem.at[buf_slot]).start()
        return [functools.partial(start, r) for r in range(bm)]

    def scatter_rows(idx_ref, buf_slot):
        def start(r):
            pltpu.make_async_copy(ybuf.at[buf_slot, pl.ds(r, 1), :],
                                  y_hbm.at[pl.ds(idx_ref[0, 0, r], 1), :], ssem.at[buf_slot]).start()
        return [functools.partial(start, r) for r in range(bm)]

    def wait_block(sem, buf):
        pltpu.make_async_copy(buf, buf, sem).wait()

    def ffn(slot, starts):
        if ff % (2 * lane) == 0:
            w_cols = [lambda: wg_ref[...].astype(BF16), lambda: wu_ref[...].astype(BF16)]
        else:
            w_cols = [lambda: wg_ref[:, :ff - lane].astype(BF16),
                      lambda: jnp.concatenate([wg_ref[:, ff - lane:], wu_ref[:, :lane]], axis=1).astype(BF16),
                      lambda: wu_ref[:, lane:].astype(BF16)]
        n_dots = EXPERT_ROW_PARTS * (len(w_cols) + 1)
        share = -(-len(starts) // n_dots)
        starts = list(starts)

        def dot_with_copies(a, w):
            for start in starts[:share]:
                start()
            del starts[:share]
            return _dot(a, w())

        part = bm // EXPERT_ROW_PARTS
        for i in range(EXPERT_ROW_PARTS):
            rows = pl.ds(i * part, part)
            xb = _unpack_halves(xbuf[slot, rows, :]).astype(BF16)
            gu = jnp.concatenate([dot_with_copies(xb, w) for w in w_cols], axis=1)
            gate, up = gu[:, :ff], gu[:, ff:]
            hid = (gate * _sigmoid(gate)) * up
            y = dot_with_copies(hid.astype(BF16), lambda: wd_ref[...].astype(BF16))
            ybuf[slot, rows, :] = _pack_halves(y * w_ref[rows, :])
        assert not starts

    @pl.when(live & (b == 0))
    def _():
        ybuf[...] = jnp.zeros_like(ybuf)
        for p in range(2):
            spare = pltpu.make_async_copy(ybuf.at[p], y_hbm.at[pl.ds(n_rows + p * bm, bm), :], ssem.at[p])
            spare.start()
            spare.wait()
        for start in gather_rows(src_ref, 0):
            start()
        wait_block(gsem.at[0], xbuf.at[0])
        ffn(0, gather_rows(nsrc_ref, 1))

    for slot in range(2):
        @pl.when(live & (b >= 1) & (b % 2 == slot))
        def _(slot=slot):
            wait_block(gsem.at[slot], xbuf.at[slot])
            pl.when(b >= 2)(lambda: wait_block(ssem.at[slot], ybuf.at[slot]))
            ffn(slot, gather_rows(nsrc_ref, 1 - slot) + scatter_rows(pdst_ref, 1 - slot))

        @pl.when(jnp.logical_not(live) & prev_live & (b % 2 == slot))
        def _(slot=slot):
            wait_block(gsem.at[slot], xbuf.at[slot])
            pl.when(b >= 2)(lambda: wait_block(ssem.at[slot], ybuf.at[slot]))
            for start in scatter_rows(pdst_ref, 1 - slot):
                start()
            wait_block(ssem.at[1 - slot], ybuf.at[1 - slot])


def _moe_experts(xn, expert, weight, w_gate, w_up, w_down, layer, *, bm):
    S, half = xn.shape
    D = 2 * half
    FF = w_down.shape[2]
    assert FF % 128 == 0
    be, live, src, dst, sw, n_blocks = _dispatch(expert, weight, bm, S)
    live = jnp.concatenate([live, jnp.zeros((1,), jnp.int32)])
    last = n_blocks - 1
    at = lambda off: (lambda b, be, live: (jnp.clip(b + off, 0, last), 0, 0))
    smem = lambda off: pl.BlockSpec((1, 1, bm), at(off), memory_space=pltpu.SMEM)
    wspec = lambda shape: pl.BlockSpec((None, None) + shape,
                                       lambda b, be, live: (layer, be[jnp.minimum(b, last)], 0, 0))
    return pl.pallas_call(
        functools.partial(_expert_kernel, bm=bm, n_rows=MOE_TOPK * S),
        grid_spec=pltpu.PrefetchScalarGridSpec(
            num_scalar_prefetch=2, grid=(n_blocks + 1,),
            in_specs=[smem(0), smem(1), smem(-1),
                      pl.BlockSpec((bm, 1), lambda b, be, live: (jnp.minimum(b, last), 0)),
                      pl.BlockSpec(memory_space=pl.ANY),
                      wspec((D, FF)), wspec((D, FF)), wspec((FF, D))],
            out_specs=pl.BlockSpec(memory_space=pl.ANY),
            scratch_shapes=[pltpu.VMEM((2, bm, half), jnp.uint32), pltpu.VMEM((2, bm, half), jnp.uint32),
                            pltpu.SemaphoreType.DMA((2,)), pltpu.SemaphoreType.DMA((2,))]),
        out_shape=jax.ShapeDtypeStruct((MOE_TOPK * S + 2 * bm, half), jnp.uint32),
        compiler_params=_params("arbitrary"), name="moe_experts",
    )(be, live, src, src, dst, sw, xn, w_gate, w_up, w_down)


def _moe(h, gain, w_group, b_group, w_expert, b_expert, w_gate, w_up, w_down, layer, *, bm):
    xn, expert, weight = _router(h, gain, w_group, b_group, w_expert, b_expert)
    return _moe_experts(xn, expert, weight, w_gate, w_up, w_down, layer, bm=bm)


def _forward(x, ln_mix, ln_ffn, hg_w_in, hg_lower_bound, hg_out_norm, hg_w_o, kv_norm, w_kv,
             da_w_q, da_lambda_q1, da_lambda_k1, da_lambda_q2, da_lambda_k2, da_subln, da_w_o,
             rel_bias, moe_w_group, moe_b_group, moe_w_expert, moe_b_expert, moe_w_gate,
             moe_w_up, moe_w_down, final_norm, *, attn_tile, moe_block):
    B, S, D = x.shape
    assert B == 1 and ln_mix.shape[0] == 2 and hg_w_in.shape[0] == 1 and da_w_q.shape[0] == 1
    assert D % (2 * HEAD_DIM) == 0 and S % attn_tile == 0 and attn_tile % REL_MAX_DIST == 0
    assert S % min(NORM_ROWS, S) == 0 and S % min(HG_TILE, S) == 0 and S % min(MM_TILE_M, S) == 0
    assert (S * MOE_TOPK) % moe_block == 0
    h = x.reshape(S, D)

    (xn,) = _norm(h, ln_mix[0:1], [BF16])
    proj = _matmul(xn, hg_w_in, layer=0, out_dtype=BF16, name="hg_in_proj")
    mix = _hgrn2(proj, hg_lower_bound, hg_out_norm[0])
    h = _matmul(mix, hg_w_o, layer=0, out_dtype=F32, res=h, name="hg_out_proj")
    wg, wu, wd = moe_w_gate, moe_w_up, moe_w_down
    y = _moe(h, ln_ffn[0], moe_w_group[0], moe_b_group[0], moe_w_expert[0], moe_b_expert[0],
             wg, wu, wd, 0, bm=moe_block)

    layer = 1
    lambda_init = 0.8 - 0.6 * math.exp(-0.3 * layer)
    h, xkv, xq = _norm(h, jnp.stack([kv_norm, ln_mix[1]]), [BF16, BF16], y=y, emit_sum=True)
    kv = _matmul(xkv, w_kv, out_dtype=BF16, name="kv_proj")
    q = _matmul(xq, da_w_q, layer=0, out_dtype=BF16, scale=HEAD_DIM ** -0.5 * LOG2E, name="q_proj")
    lam_params = jnp.stack([da_lambda_q1[0], da_lambda_k1[0], da_lambda_q2[0], da_lambda_k2[0]])
    att = _diff_attention(q, kv, _attn_bias_tiles(rel_bias, attn_tile), lam_params, da_subln[0],
                          lambda_init, t=attn_tile)
    h = _matmul(att, da_w_o, layer=0, out_dtype=F32, res=h, name="da_out_proj")
    y = _moe(h, ln_ffn[1], moe_w_group[1], moe_b_group[1], moe_w_expert[1], moe_b_expert[1],
             wg, wu, wd, 1, bm=moe_block)

    (out,) = _norm(h, final_norm.reshape(1, D), [F32], y=y)
    return out.reshape(B, S, D)


def kernel(x, ln_mix, ln_ffn, hg_w_in, hg_lower_bound, hg_out_norm, hg_w_o, kv_norm, w_kv, da_w_q, da_lambda_q1, da_lambda_k1, da_lambda_q2, da_lambda_k2, da_subln, da_w_o, rel_bias, moe_w_group, moe_b_group, moe_w_expert, moe_b_expert, moe_w_gate, moe_w_up, moe_w_down, final_norm):
    return _forward(x, ln_mix, ln_ffn, hg_w_in, hg_lower_bound, hg_out_norm, hg_w_o, kv_norm, w_kv,
                    da_w_q, da_lambda_q1, da_lambda_k1, da_lambda_q2, da_lambda_k2, da_subln,
                    da_w_o, rel_bias, moe_w_group, moe_b_group, moe_w_expert, moe_b_expert,
                    moe_w_gate, moe_w_up, moe_w_down, final_norm,
                    attn_tile=min(ATTN_TILE, x.shape[1]), moe_block=MOE_BLOCK)
```
